```python
import math
import jax, jax.numpy as jnp
from jax import lax
import numpy as np

D_MODEL = 1024
BATCH = 8
SEQ = 2048
DEPTH = 2
DEC_BATCH = 32
DEC_SEQ = 8
PAST_LEN = 8192
PAGE_SIZE = 128

N_A_LAYERS = DEPTH // 2
N_B_LAYERS = DEPTH - N_A_LAYERS
RET_HEADS = 4
RET_DK = D_MODEL // RET_HEADS
RET_QK = RET_HEADS * RET_DK
RET_V = 2 * D_MODEL
RET_DV = RET_V // RET_HEADS
RET_CHUNK = 128
ATTN_HEADS = 8
ATTN_DH = D_MODEL // ATTN_HEADS
MOBA_BLOCK = 256
MOBA_TOPK = 3
MOBA_QCHUNK = 16
D_FF = 4 * D_MODEL
NORM_EPS = 1e-5
GN_EPS = 1e-6
NEG = -1e30

kernel_name = 'yoco_retnet_moba_decode_step'


def rms_norm(x, g, eps=NORM_EPS):
    xf = x.astype(jnp.float32)
    y = xf * lax.rsqrt(jnp.mean(xf * xf, axis=-1, keepdims=True) + eps)
    return (y * g.astype(jnp.float32)).astype(x.dtype)


def squared_relu_mlp(h, g, w_up, w_down):
    u = jax.nn.relu(rms_norm(h, g) @ w_up)
    return h + (u * u) @ w_down


def _ret_log_decay():
    return jnp.log(1.0 - 2.0 ** (-5.0 - jnp.arange(RET_HEADS, dtype=jnp.float32)))


def _ret_rotary(pos):
    angle = 1.0 / (10000.0 ** jnp.linspace(0.0, 1.0, RET_DK // 2, dtype=jnp.float32))
    angle = jnp.repeat(angle, 2)
    ang = pos.astype(jnp.float32)[:, None] * angle[None, :]
    return jnp.sin(ang)[:, None, :], jnp.cos(ang)[:, None, :]


def _theta_shift(x, sin, cos):
    x1 = x[..., ::2]
    x2 = x[..., 1::2]
    rot = jnp.stack((-x2, x1), axis=-1).reshape(x.shape)
    return x * cos + rot * sin


def retention_chunked(q, k, v, s0, chunk):
    B, L, H, _ = q.shape
    n = L // chunk
    lg = _ret_log_decay()
    i = jnp.arange(chunk, dtype=jnp.float32)
    diff = i[:, None] - i[None, :]
    d_intra = jnp.exp(jnp.where(diff >= 0, lg[:, None, None] * diff, -jnp.inf))
    q_dec = jnp.exp(lg[:, None] * (i + 1.0))[..., None]
    k_dec = jnp.exp(lg[:, None] * (chunk - 1.0 - i))[..., None]
    c_dec = jnp.exp(lg * chunk)[:, None, None]

    def to_chunks(t):
        return t.reshape(B, n, chunk, H, t.shape[-1]).transpose(1, 0, 3, 2, 4)

    def step(s, xs):
        qc, kc, vc = xs
        a = jnp.einsum('bhid,bhjd->bhij', qc, kc) * d_intra
        o = jnp.einsum('bhij,bhjv->bhiv', a, vc) + jnp.einsum('bhid,bhdv->bhiv', qc * q_dec, s)
        s = s * c_dec + jnp.einsum('bhjd,bhjv->bhdv', kc * k_dec, vc)
        return s, o

    s, o = lax.scan(step, s0, (to_chunks(q), to_chunks(k), to_chunks(v)))
    return o.transpose(1, 0, 3, 2, 4).reshape(B, L, H, -1), s


def retention_layer(x, s0, pos, g, w_in, w_out):
    B, L, _ = x.shape
    proj = (rms_norm(x, g) @ w_in).astype(jnp.float32)
    q, k, v, gate = jnp.split(proj, [RET_QK, 2 * RET_QK, 2 * RET_QK + RET_V], axis=-1)
    sin, cos = _ret_rotary(pos)
    q = _theta_shift(q.reshape(B, L, RET_HEADS, RET_DK), sin, cos)
    k = _theta_shift(k.reshape(B, L, RET_HEADS, RET_DK), sin, cos) * (RET_DK ** -0.5)
    v = v.reshape(B, L, RET_HEADS, RET_DV)
    o, s = retention_chunked(q, k, v, s0.astype(jnp.float32), math.gcd(L, RET_CHUNK))
    o = o * lax.rsqrt(jnp.mean(o * o, axis=-1, keepdims=True) + GN_EPS)
    o = o.reshape(B, L, RET_V) * jax.nn.silu(gate)
    return x + o.astype(x.dtype) @ w_out, s


def shared_kv(h, g, w_kv):
    B, L, _ = h.shape
    kv = rms_norm(h, g) @ w_kv
    k = kv[..., :D_MODEL].reshape(B, L, ATTN_HEADS, ATTN_DH)
    v = kv[..., D_MODEL:].reshape(B, L, ATTN_HEADS, ATTN_DH)
    return k, v


def _pad_blocks(means, nbp):
    return jnp.pad(means, ((0, 0), (0, nbp - means.shape[1]), (0, 0), (0, 0)))


def _select_blocks(q, means, own_block):
    s = jnp.einsum('bqhd,bnhd->bqhn', q.astype(jnp.float32), means)
    nbp = means.shape[1]
    fully_past = jnp.arange(nbp)[None, None, None, :] < own_block[None, :, None, None]
    vals, idx = lax.top_k(jnp.where(fully_past, s, NEG), MOBA_TOPK)
    return idx, vals > 0.5 * NEG


def _masked_softmax(logits, mask):
    return jax.nn.softmax(jnp.where(mask, logits.astype(jnp.float32) * ATTN_DH ** -0.5, NEG), axis=-1)


def moba_prompt(q, k, v):
    B, S, H, Dh = q.shape
    nb = -(-S // MOBA_BLOCK)
    pad = nb * MOBA_BLOCK - S
    kb = jnp.pad(k, ((0, 0), (0, pad), (0, 0), (0, 0))).reshape(B, nb, MOBA_BLOCK, H, Dh)
    vb = jnp.pad(v, ((0, 0), (0, pad), (0, 0), (0, 0))).reshape(B, nb, MOBA_BLOCK, H, Dh)
    means = _pad_blocks(jnp.mean(kb.astype(jnp.float32), axis=2), max(nb, MOBA_TOPK))
    kbt = kb.transpose(0, 1, 3, 2, 4)
    vbt = vb.transpose(0, 1, 3, 2, 4)
    nq = S // MOBA_QCHUNK
    qs = q.reshape(B, nq, MOBA_QCHUNK, H, Dh).transpose(1, 0, 2, 3, 4)
    bi = jnp.arange(B)[:, None, None, None]
    hi = jnp.arange(H)[None, None, :, None]
    r = jnp.arange(MOBA_BLOCK)
    nsel = MOBA_TOPK * MOBA_BLOCK

    def one_chunk(args):
        qc, c = args
        start = c * MOBA_QCHUNK
        qpos = start + jnp.arange(MOBA_QCHUNK)
        idx, ok = _select_blocks(qc, means, qpos // MOBA_BLOCK)
        idx_c = jnp.minimum(idx, nb - 1)
        k_sel = kbt[bi, idx_c, hi]
        v_sel = vbt[bi, idx_c, hi]
        own = start // MOBA_BLOCK
        k_own = lax.dynamic_index_in_dim(kb, own, axis=1, keepdims=False)
        v_own = lax.dynamic_index_in_dim(vb, own, axis=1, keepdims=False)
        l_sel = jnp.einsum('bqhd,bqhnrd->bqhnr', qc, k_sel)
        l_own = jnp.einsum('bqhd,brhd->bqhr', qc, k_own)
        own_ok = (own * MOBA_BLOCK + r)[None, :] <= qpos[:, None]
        logits = jnp.concatenate([l_sel.reshape(B, MOBA_QCHUNK, H, nsel), l_own], axis=-1)
        mask = jnp.concatenate([jnp.repeat(ok, MOBA_BLOCK, axis=-1),
                                jnp.broadcast_to(own_ok[None, :, None, :], l_own.shape)], axis=-1)
        p = _masked_softmax(logits, mask)
        p_sel = p[..., :nsel].reshape(l_sel.shape)
        p_own = p[..., nsel:]
        out = (jnp.einsum('bqhnr,bqhnrd->bqhd', p_sel, v_sel.astype(jnp.float32))
               + jnp.einsum('bqhr,brhd->bqhd', p_own, v_own.astype(jnp.float32)))
        return out.astype(q.dtype)

    o = lax.map(one_chunk, (qs, jnp.arange(nq)))
    return o.transpose(1, 0, 2, 3, 4).reshape(B, S, H, Dh)


def moba_sample(q, k_new, v_new, cache_k, cache_v, page_table):
    B, L, H, Dh = q.shape
    n_pages = PAST_LEN // PAGE_SIZE
    past = n_pages * PAGE_SIZE
    nb = -(-(past + L) // MOBA_BLOCK)
    ppb = MOBA_BLOCK // PAGE_SIZE
    page_sum = jnp.sum(cache_k[page_table].astype(jnp.float32), axis=2)
    page_sum = jnp.pad(page_sum, ((0, 0), (0, nb * ppb - n_pages), (0, 0), (0, 0)))
    blk = page_sum.reshape(B, nb, ppb, H, Dh).sum(axis=2)
    qpos = past + jnp.arange(L, dtype=jnp.int32)
    blk = blk.at[:, qpos // MOBA_BLOCK].add(k_new.astype(jnp.float32))
    means = _pad_blocks(blk / MOBA_BLOCK, max(nb, MOBA_TOPK))
    bi = jnp.arange(B)[:, None, None, None]
    hi = jnp.arange(H)[None, :, None, None]
    r = jnp.arange(MOBA_BLOCK)

    def one_query(args):
        qt, t = args
        own = t // MOBA_BLOCK
        idx, ok = _select_blocks(qt[:, None], means, own[None])
        blocks = jnp.concatenate([idx[:, 0], jnp.full((B, H, 1), own, idx.dtype)], axis=-1)
        slot_ok = jnp.concatenate([ok[:, 0], jnp.ones((B, H, 1), bool)], axis=-1)
        p = blocks[..., None] * MOBA_BLOCK + r
        phys = page_table[bi, jnp.clip(p // PAGE_SIZE, 0, n_pages - 1)]
        off = p % PAGE_SIZE
        j = jnp.clip(p - past, 0, L - 1)
        in_past = (p < past)[..., None]
        k_g = jnp.where(in_past, cache_k[phys, off, hi], k_new[bi, j, hi])
        v_g = jnp.where(in_past, cache_v[phys, off, hi], v_new[bi, j, hi])
        logits = jnp.einsum('bhd,bhnrd->bhnr', qt, k_g)
        mask = slot_ok[..., None] & (p <= t)
        w = _masked_softmax(logits.reshape(B, H, -1), mask.reshape(B, H, -1)).reshape(logits.shape)
        return jnp.einsum('bhnr,bhnrd->bhd', w, v_g.astype(jnp.float32)).astype(q.dtype)

    o = lax.map(one_query, (q.transpose(1, 0, 2, 3), qpos))
    return o.transpose(1, 0, 2, 3)


def setup_inputs(seed: int = 0) -> dict:
    key = jax.random.key(seed)
    ks = jax.random.split(key, 20)
    f32 = jnp.float32
    n_pages = PAST_LEN // PAGE_SIZE
    n_phys = (DEC_BATCH * n_pages * 5) // 4

    def w(k, shape, fan_in):
        return jax.random.normal(k, shape, f32) * fan_in ** -0.5

    def gain(k, shape):
        return 1.0 + 0.02 * jax.random.normal(k, shape, f32)

    x_prompt = jax.random.normal(ks[0], (BATCH, SEQ, D_MODEL), f32)
    x_sample = jax.random.normal(ks[1], (DEC_BATCH, DEC_SEQ, D_MODEL), f32)
    state_ret = 0.5 * jax.random.normal(ks[2], (N_A_LAYERS, DEC_BATCH, RET_HEADS, RET_DK, RET_DV), f32)
    cache_k = jax.random.normal(ks[3], (n_phys, PAGE_SIZE, ATTN_HEADS, ATTN_DH), f32)
    cache_v = jax.random.normal(ks[4], (n_phys, PAGE_SIZE, ATTN_HEADS, ATTN_DH), f32)
    page_table = jax.random.permutation(ks[5], n_phys)[: DEC_BATCH * n_pages].reshape(DEC_BATCH, n_pages).astype(jnp.int32)
    return {
        'x_prompt': x_prompt,
        'x_sample': x_sample,
        'state_ret': state_ret,
        'cache_k': cache_k,
        'cache_v': cache_v,
        'page_table': page_table,
        'ret_norm_g': gain(ks[6], (N_A_LAYERS, D_MODEL)),
        'ret_w_in': w(ks[7], (N_A_LAYERS, D_MODEL, 2 * RET_QK + 2 * RET_V), D_MODEL),
        'ret_w_out': w(ks[8], (N_A_LAYERS, RET_V, D_MODEL), RET_V),
        'attn_norm_g': gain(ks[9], (N_B_LAYERS, D_MODEL)),
        'attn_w_q': w(ks[10], (N_B_LAYERS, D_MODEL, ATTN_HEADS * ATTN_DH), D_MODEL),
        'attn_w_out': w(ks[11], (N_B_LAYERS, ATTN_HEADS * ATTN_DH, D_MODEL), ATTN_HEADS * ATTN_DH),
        'kv_norm_g': gain(ks[12], (D_MODEL,)),
        'w_kv': w(ks[13], (D_MODEL, 2 * ATTN_HEADS * ATTN_DH), D_MODEL),
        'mlp_norm_g': gain(ks[14], (DEPTH, D_MODEL)),
        'mlp_w_up': w(ks[15], (DEPTH, D_MODEL, D_FF), D_MODEL),
        'mlp_w_down': w(ks[16], (DEPTH, D_FF, D_MODEL), D_FF),
        'final_norm_g': gain(ks[17], (D_MODEL,)),
    }


def reference(x_prompt, x_sample, state_ret, cache_k, cache_v, page_table,
              ret_norm_g, ret_w_in, ret_w_out, attn_norm_g, attn_w_q, attn_w_out,
              kv_norm_g, w_kv, mlp_norm_g, mlp_w_up, mlp_w_down, final_norm_g):
    Bp, Lp, _ = x_prompt.shape
    Bs, Ls, _ = x_sample.shape
    pos_p = jnp.arange(Lp, dtype=jnp.int32)
    pos_s = PAST_LEN + jnp.arange(Ls, dtype=jnp.int32)
    hp, hs = x_prompt, x_sample
    s_prompt, s_sample = [], []
    k_prompt = v_prompt = k_sample = v_sample = None
    for layer in range(DEPTH):
        if layer < N_A_LAYERS:
            a = layer
            zeros = jnp.zeros((Bp, RET_HEADS, RET_DK, RET_DV), jnp.float32)
            hp, sp = retention_layer(hp, zeros, pos_p, ret_norm_g[a], ret_w_in[a], ret_w_out[a])
            hs, ss = retention_layer(hs, state_ret[a], pos_s, ret_norm_g[a], ret_w_in[a], ret_w_out[a])
            s_prompt.append(sp)
            s_sample.append(ss)
        else:
            b = layer - N_A_LAYERS
            if b == 0:
                k_prompt, v_prompt = shared_kv(hp, kv_norm_g, w_kv)
                k_sample, v_sample = shared_kv(hs, kv_norm_g, w_kv)
            qp = (rms_norm(hp, attn_norm_g[b]) @ attn_w_q[b]).reshape(Bp, Lp, ATTN_HEADS, ATTN_DH)
            qs = (rms_norm(hs, attn_norm_g[b]) @ attn_w_q[b]).reshape(Bs, Ls, ATTN_HEADS, ATTN_DH)
            op = moba_prompt(qp, k_prompt, v_prompt)
            os_ = moba_sample(qs, k_sample, v_sample, cache_k, cache_v, page_table)
            hp = hp + op.reshape(Bp, Lp, D_MODEL) @ attn_w_out[b]
            hs = hs + os_.reshape(Bs, Ls, D_MODEL) @ attn_w_out[b]
        hp = squared_relu_mlp(hp, mlp_norm_g[layer], mlp_w_up[layer], mlp_w_down[layer])
        hs = squared_relu_mlp(hs, mlp_norm_g[layer], mlp_w_up[layer], mlp_w_down[layer])
    y_prompt = rms_norm(hp, final_norm_g)
    y_sample = rms_norm(hs, final_norm_g)
    state_ret_prompt = jnp.stack(s_prompt)
    state_ret_sample = jnp.stack(s_sample)
    return (y_prompt, y_sample, state_ret_prompt, state_ret_sample, k_prompt, v_prompt, k_sample, v_sample)
```

```python
import functools
import math

import jax
import jax.numpy as jnp
from jax import lax
from jax.experimental import pallas as pl
from jax.experimental.pallas import tpu as pltpu

F32 = jnp.float32
BF16 = jnp.bfloat16

D_MODEL = 1024
PAST_LEN = 8192
PAGE_SIZE = 128
RET_HEADS = 4
RET_DK = D_MODEL // RET_HEADS
RET_QK = RET_HEADS * RET_DK
RET_V = 2 * D_MODEL
RET_DV = RET_V // RET_HEADS
RET_CHUNK = 128
ATTN_HEADS = 8
ATTN_DH = D_MODEL // ATTN_HEADS
MOBA_BLOCK = 256
MOBA_TOPK = 3
D_FF = 4 * D_MODEL
NORM_EPS = 1e-5
GN_EPS = 1e-6
NEG = -1e30

PAGES_PER_BLOCK = MOBA_BLOCK // PAGE_SIZE
RET_ROWS = 128
VMEM_LIMIT_BYTES = 56 * 1024 * 1024

_NT = (((1,), (1,)), ((), ()))
_TN = (((0,), (0,)), ((), ()))


def _params(*sem):
    return pltpu.CompilerParams(dimension_semantics=sem, vmem_limit_bytes=VMEM_LIMIT_BYTES)


def _rms(x, g, eps=NORM_EPS):
    return x * lax.rsqrt(jnp.mean(x * x, axis=-1, keepdims=True) + eps) * g


def _norm_matmul_kernel(x_ref, g_ref, w_ref, o_ref, xn_ref):
    @pl.when(pl.program_id(1) == 0)
    def _():
        xn_ref[...] = _rms(x_ref[...], g_ref[...]).astype(BF16)

    o_ref[...] = jnp.dot(xn_ref[...], w_ref[...], preferred_element_type=F32).astype(o_ref.dtype)


def norm_matmul(x, g, w, out_dtype, tm=1024, tn=1024):
    m, d = x.shape
    n = w.shape[1]
    tm = min(tm, m)
    return pl.pallas_call(
        _norm_matmul_kernel,
        grid=(m // tm, n // tn),
        in_specs=[pl.BlockSpec((tm, d), lambda i, j: (i, 0)),
                  pl.BlockSpec((1, d), lambda i, j: (0, 0)),
                  pl.BlockSpec((d, tn), lambda i, j: (0, j))],
        out_specs=pl.BlockSpec((tm, tn), lambda i, j: (i, j)),
        out_shape=jax.ShapeDtypeStruct((m, n), out_dtype),
        scratch_shapes=[pltpu.VMEM((tm, d), BF16)],
        compiler_params=_params("parallel", "arbitrary"),
        name="norm_matmul",
    )(x, g.reshape(1, d), w)


def _matmul_residual_kernel(a_ref, w_ref, r_ref, o_ref):
    o_ref[...] = r_ref[...] + jnp.dot(a_ref[...], w_ref[...], preferred_element_type=F32)


def matmul_residual(a, w, res, tm=512):
    m, k = a.shape
    n = w.shape[1]
    tm = min(tm, m)
    return pl.pallas_call(
        _matmul_residual_kernel,
        grid=(m // tm,),
        in_specs=[pl.BlockSpec((tm, k), lambda i: (i, 0)),
                  pl.BlockSpec((k, n), lambda i: (0, 0)),
                  pl.BlockSpec((tm, n), lambda i: (i, 0))],
        out_specs=pl.BlockSpec((tm, n), lambda i: (i, 0)),
        out_shape=jax.ShapeDtypeStruct((m, n), F32),
        compiler_params=_params("parallel"),
        name="matmul_residual",
    )(a, w, res)


def _mlp_kernel(x_ref, g_ref, wu_ref, wd_ref, gf_ref, o_ref, xn_ref, acc_ref, *, final_norm):
    f = pl.program_id(1)

    @pl.when(f == 0)
    def _():
        xn_ref[...] = _rms(x_ref[...], g_ref[...]).astype(BF16)
        acc_ref[...] = jnp.zeros_like(acc_ref)

    u = jnp.maximum(jnp.dot(xn_ref[...], wu_ref[...], preferred_element_type=F32), 0.0)
    acc_ref[...] += jnp.dot((u * u).astype(BF16), wd_ref[...], preferred_element_type=F32)

    @pl.when(f == pl.num_programs(1) - 1)
    def _():
        h = x_ref[...] + acc_ref[...]
        o_ref[...] = _rms(h, gf_ref[...]) if final_norm else h


def mlp(x, g, w_up, w_down, g_final=None, tm=512, tf=1024):
    m, d = x.shape
    ff = w_up.shape[1]
    tm = min(tm, m)
    final_norm = g_final is not None
    gf = g_final if final_norm else g
    return pl.pallas_call(
        functools.partial(_mlp_kernel, final_norm=final_norm),
        grid=(m // tm, ff // tf),
        in_specs=[pl.BlockSpec((tm, d), lambda i, f: (i, 0)),
                  pl.BlockSpec((1, d), lambda i, f: (0, 0)),
                  pl.BlockSpec((d, tf), lambda i, f: (0, f)),
                  pl.BlockSpec((tf, d), lambda i, f: (f, 0)),
                  pl.BlockSpec((1, d), lambda i, f: (0, 0))],
        out_specs=pl.BlockSpec((tm, d), lambda i, f: (i, 0)),
        out_shape=jax.ShapeDtypeStruct((m, d), F32),
        scratch_shapes=[pltpu.VMEM((tm, d), BF16), pltpu.VMEM((tm, d), F32)],
        compiler_params=_params("parallel", "arbitrary"),
        name="mlp",
    )(x, g.reshape(1, d), w_up, w_down, gf.reshape(1, d))


def _kvq_kernel(x_ref, gkv_ref, gq_ref, wk_ref, wv_ref, wq_ref, k_ref, v_ref, q_ref):
    x = x_ref[...]
    r = x * lax.rsqrt(jnp.mean(x * x, axis=-1, keepdims=True) + NORM_EPS)
    xkv = (r * gkv_ref[...]).astype(BF16)
    xq = (r * gq_ref[...]).astype(BF16)
    k_ref[...] = jnp.dot(xkv, wk_ref[...], preferred_element_type=F32)
    v_ref[...] = jnp.dot(xkv, wv_ref[...], preferred_element_type=F32)
    q_ref[...] = jnp.dot(xq, wq_ref[...], preferred_element_type=F32)


def kvq_proj(x, g_kv, g_q, wk, wv, wq, tm=512):
    m, d = x.shape
    tm = min(tm, m)
    row = pl.BlockSpec((tm, d), lambda i: (i, 0))
    gain = pl.BlockSpec((1, d), lambda i: (0, 0))
    weight = pl.BlockSpec((d, d), lambda i: (0, 0))
    out = jax.ShapeDtypeStruct((m, d), F32)
    return pl.pallas_call(
        _kvq_kernel,
        grid=(m // tm,),
        in_specs=[row, gain, gain, weight, weight, weight],
        out_specs=[row, row, row],
        out_shape=[out, out, out],
        compiler_params=_params("parallel"),
        name="kvq_proj",
    )(x, g_kv.reshape(1, d), g_q.reshape(1, d), wk, wv, wq)


def _ret_log_decay():
    return jnp.log(1.0 - 2.0 ** (-5.0 - jnp.arange(RET_HEADS, dtype=F32)))


def _ret_tables(pos, chunk):
    angle = 1.0 / (10000.0 ** jnp.linspace(0.0, 1.0, RET_DK // 2, dtype=F32))
    angle = jnp.repeat(angle, 2)
    ang = pos.astype(F32)[:, None] * angle[None, :]
    sin, cos = jnp.sin(ang), jnp.cos(ang)
    even = (jnp.arange(RET_DK) % 2 == 0)[None, :]
    sin_next = jnp.where(even, -sin, 0.0)
    sin_prev = jnp.where(even, 0.0, sin)

    lg = _ret_log_decay()
    i = jnp.arange(chunk, dtype=F32)
    diff = i[:, None] - i[None, :]
    d_intra = jnp.exp(jnp.where(diff >= 0, lg[:, None, None] * diff, -jnp.inf))
    q_dec = jnp.exp(lg[:, None] * (i + 1.0))
    k_dec = jnp.exp(lg[:, None] * (chunk - 1.0 - i))
    c_dec = jnp.exp(lg * chunk)
    pad = RET_ROWS - chunk
    d_intra = jnp.pad(d_intra, ((0, 0), (0, pad), (0, pad)))
    q_dec = jnp.broadcast_to(jnp.pad(q_dec, ((0, 0), (0, pad)))[:, :, None], (RET_HEADS, RET_ROWS, RET_DK))
    k_dec = jnp.broadcast_to(jnp.pad(k_dec, ((0, 0), (0, pad)))[:, :, None], (RET_HEADS, RET_ROWS, RET_DK))
    c_dec = jnp.broadcast_to(c_dec[:, None, None], (RET_HEADS, 1, RET_DV))
    return cos, sin_next, sin_prev, d_intra, q_dec, k_dec, c_dec


def _theta_shift(x, cos, sin_next, sin_prev):
    dk = x.shape[-1]
    return x * cos + pltpu.roll(x, dk - 1, 1) * sin_next + pltpu.roll(x, 1, 1) * sin_prev


def _pad_rows(x, rows):
    if x.shape[0] == rows:
        return x
    return jnp.concatenate([x, jnp.zeros((rows - x.shape[0], x.shape[1]), x.dtype)], axis=0)


def _retention_kernel(q_ref, k_ref, v_ref, gate_ref, cos_ref, sn_ref, sp_ref, dintra_ref, qdec_ref, kdec_ref,
                      cdec_ref, s0_ref, o_ref, sout_ref, s_ref, *, chunk, has_state):
    c = pl.program_id(2)

    @pl.when(c == 0)
    def _():
        s_ref[...] = s0_ref[0, 0] if has_state else jnp.zeros_like(s_ref)

    cos, sn, sp = cos_ref[...], sn_ref[...], sp_ref[...]
    q = _theta_shift(q_ref[0].astype(F32), cos, sn, sp)
    k = _theta_shift(k_ref[0].astype(F32), cos, sn, sp) * (RET_DK ** -0.5)
    q = _pad_rows(q, RET_ROWS)
    k = _pad_rows(k, RET_ROWS)
    v = _pad_rows(v_ref[0], RET_ROWS).astype(BF16)
    s = s_ref[...]

    a = lax.dot_general(q.astype(BF16), k.astype(BF16), _NT, preferred_element_type=F32) * dintra_ref[0]
    o = (jnp.dot(a.astype(BF16), v, preferred_element_type=F32)
         + jnp.dot((q * qdec_ref[0]).astype(BF16), s.astype(BF16), preferred_element_type=F32))
    s_new = s * cdec_ref[0] + lax.dot_general((k * kdec_ref[0]).astype(BF16), v, _TN, preferred_element_type=F32)
    s_ref[...] = s_new

    o = o[:chunk]
    o = o * lax.rsqrt(jnp.mean(o * o, axis=-1, keepdims=True) + GN_EPS)
    gate = gate_ref[0].astype(F32)
    o_ref[0] = (o * (gate * jax.nn.sigmoid(gate))).astype(o_ref.dtype)

    @pl.when(c == pl.num_programs(2) - 1)
    def _():
        sout_ref[0, 0] = s_new


def retention(proj, pos, s0):
    b, l, _ = proj.shape
    chunk = math.gcd(l, RET_CHUNK)
    n = l // chunk
    has_state = s0 is not None
    if not has_state:
        s0 = jnp.zeros((1, 1, RET_DK, RET_DV), F32)
    tables = _ret_tables(pos, chunk)
    kq, kv = RET_QK // RET_DK, 2 * RET_QK // RET_DV
    kg = kv + RET_V // RET_DV
    pos_tab = pl.BlockSpec((chunk, RET_DK), lambda bi, h, c: (c, 0))
    head_tab = lambda rows, cols: pl.BlockSpec((1, rows, cols), lambda bi, h, c: (h, 0, 0))
    state_in = (pl.BlockSpec((1, 1, RET_DK, RET_DV), lambda bi, h, c: (bi, h, 0, 0)) if has_state
                else pl.BlockSpec((1, 1, RET_DK, RET_DV), lambda bi, h, c: (0, 0, 0, 0)))
    return pl.pallas_call(
        functools.partial(_retention_kernel, chunk=chunk, has_state=has_state),
        grid=(b, RET_HEADS, n),
        in_specs=[pl.BlockSpec((1, chunk, RET_DK), lambda bi, h, c: (bi, c, h)),
                  pl.BlockSpec((1, chunk, RET_DK), lambda bi, h, c: (bi, c, kq + h)),
                  pl.BlockSpec((1, chunk, RET_DV), lambda bi, h, c: (bi, c, kv + h)),
                  pl.BlockSpec((1, chunk, RET_DV), lambda bi, h, c: (bi, c, kg + h)),
                  pos_tab, pos_tab, pos_tab,
                  head_tab(RET_ROWS, RET_ROWS), head_tab(RET_ROWS, RET_DK), head_tab(RET_ROWS, RET_DK),
                  head_tab(1, RET_DV), state_in],
        out_specs=[pl.BlockSpec((1, chunk, RET_DV), lambda bi, h, c: (bi, c, h)),
                   pl.BlockSpec((1, 1, RET_DK, RET_DV), lambda bi, h, c: (bi, h, 0, 0))],
        out_shape=[jax.ShapeDtypeStruct((b, l, RET_V), proj.dtype),
                   jax.ShapeDtypeStruct((b, RET_HEADS, RET_DK, RET_DV), F32)],
        scratch_shapes=[pltpu.VMEM((RET_DK, RET_DV), F32)],
        compiler_params=_params("parallel", "parallel", "arbitrary"),
        name="retention",
    )(proj, proj, proj, proj, *tables, s0)


def _moba_prompt_kernel(q_ref, k_ref, v_ref, o_ref, *, nb):
    k = k_ref[0]
    kb = k.astype(BF16)
    vb = v_ref[0].astype(BF16)
    means = jnp.sum(k.reshape(nb, MOBA_BLOCK, ATTN_DH), axis=1) * (1.0 / MOBA_BLOCK)
    means = _pad_rows(means, 128)
    lane = lax.broadcasted_iota(jnp.int32, (MOBA_BLOCK, 128), 1)
    row = lax.broadcasted_iota(jnp.int32, (MOBA_BLOCK, MOBA_BLOCK), 0)
    col = lax.broadcasted_iota(jnp.int32, (MOBA_BLOCK, MOBA_BLOCK), 1)
    causal = col <= row

    for i in range(nb):
        q = q_ref[0, i * MOBA_BLOCK:(i + 1) * MOBA_BLOCK, :]
        qb = (q * ATTN_DH ** -0.5).astype(BF16)
        kv_rows = (i + 1) * MOBA_BLOCK
        logits = lax.dot_general(qb, kb[:kv_rows], _NT, preferred_element_type=F32)
        pieces = [jnp.where(causal, logits[:, i * MOBA_BLOCK:], NEG)]
        if i > 0:
            s = lax.dot_general(q, means, _NT, preferred_element_type=F32, precision=lax.Precision.HIGHEST)
            elig = lane < i
            sm = jnp.where(elig, s, NEG)
            rank = jnp.zeros(sm.shape, F32)
            for m in range(i):
                cm = sm[:, m:m + 1]
                beats = jnp.where(cm > sm, 1.0, jnp.where(cm == sm, jnp.where(lane > m, 1.0, 0.0), 0.0))
                rank = rank + beats
            sel = jnp.where(elig, jnp.where(rank < MOBA_TOPK, jnp.where(sm > 0.5 * NEG, 1.0, 0.0), 0.0), 0.0)
            for n in range(i):
                pieces.append(jnp.where(sel[:, n:n + 1] > 0.5, logits[:, n * MOBA_BLOCK:(n + 1) * MOBA_BLOCK], NEG))
        mx = functools.reduce(jnp.maximum, [jnp.max(p, axis=-1, keepdims=True) for p in pieces])
        probs = [jnp.exp(p - mx) for p in pieces]
        denom = functools.reduce(lambda x, y: x + y, [jnp.sum(p, axis=-1, keepdims=True) for p in probs])
        blocks = [i] + list(range(i))
        acc = functools.reduce(lambda x, y: x + y, [
            jnp.dot(p.astype(BF16), vb[n * MOBA_BLOCK:(n + 1) * MOBA_BLOCK], preferred_element_type=F32)
            for p, n in zip(probs, blocks)])
        o_ref[0, i * MOBA_BLOCK:(i + 1) * MOBA_BLOCK, :] = (acc / denom).astype(o_ref.dtype)


def moba_prompt(q, k, v):
    b, s, _ = q.shape
    assert s % MOBA_BLOCK == 0
    nb = s // MOBA_BLOCK
    spec = pl.BlockSpec((1, s, ATTN_DH), lambda bi, h: (bi, 0, h))
    return pl.pallas_call(
        functools.partial(_moba_prompt_kernel, nb=nb),
        grid=(b, ATTN_HEADS),
        in_specs=[spec, spec, spec],
        out_specs=spec,
        out_shape=jax.ShapeDtypeStruct((b, s, ATTN_HEADS * ATTN_DH), BF16),
        compiler_params=_params("parallel", "parallel"),
        name="moba_prompt",
    )(q, k, v)


def _block_diag_queries(q):
    b, l, h, dh = q.shape
    eye = jnp.eye(h, dtype=q.dtype)
    w = q.transpose(0, 2, 1, 3)[:, :, :, None, :] * eye[None, :, None, :, None]
    return w.reshape(b, h * l, h * dh)


def _moba_sample_keys_kernel(pt_ref, wq_ref, k0_ref, k1_ref, bsum_ref, lg_ref):
    del pt_ref
    w = wq_ref[0]
    for j, kref in enumerate((k0_ref, k1_ref)):
        kp = kref[0]
        psum = jnp.sum(kp, axis=0, keepdims=True)
        bsum_ref[0, 0] = psum if j == 0 else bsum_ref[0, 0] + psum
        lg_ref[0, 0, :, j * PAGE_SIZE:(j + 1) * PAGE_SIZE] = lax.dot_general(
            w, kp.astype(BF16), _NT, preferred_element_type=F32)


def _moba_sample_values_kernel(pt_ref, wq_ref, wqs_ref, bsum_ref, kn_ref, vn_ref, lg_ref, v0_ref, v1_ref, o_ref,
                               sel_ref, m_ref, l_ref, acc_ref, *, n_new, n_past_blocks):
    del pt_ref
    n = pl.program_id(1)
    rows = wq_ref.shape[1]
    lane = lax.broadcasted_iota(jnp.int32, (rows, 128), 1)

    @pl.when(n == 0)
    def _():
        kn = kn_ref[0]
        first_row = lax.broadcasted_iota(jnp.int32, (8, kn.shape[1]), 0) == 0
        own_sum = jnp.where(first_row, jnp.sum(kn, axis=0, keepdims=True), 0.0)
        means = _pad_rows(jnp.concatenate([bsum_ref[0], own_sum], axis=0) * (1.0 / MOBA_BLOCK), 128)
        s = lax.dot_general(wq_ref[0], means, _NT, preferred_element_type=F32, precision=lax.Precision.HIGHEST)
        work = jnp.where(lane < n_past_blocks, s, NEG)
        sel = jnp.zeros(work.shape, F32)
        for _ in range(MOBA_TOPK):
            mx = jnp.max(work, axis=-1, keepdims=True)
            first = jnp.min(jnp.where(work == mx, lane, 128), axis=-1, keepdims=True)
            pick = lane == first
            sel = jnp.where(pick, jnp.where(mx > 0.5 * NEG, 1.0, 0.0), sel)
            work = jnp.where(pick, -jnp.inf, work)
        sel_ref[...] = sel
        t = lax.broadcasted_iota(jnp.int32, (rows, 128), 0) % n_new
        lo = lax.dot_general(wqs_ref[0], _pad_rows(kn, 128).astype(BF16), _NT, preferred_element_type=F32)
        lo = jnp.where(lane <= t, lo, NEG)
        m0 = jnp.max(lo, axis=-1, keepdims=True)
        p = jnp.exp(lo - m0)
        m_ref[...] = jnp.broadcast_to(m0, m_ref.shape)
        l_ref[...] = jnp.broadcast_to(jnp.sum(p, axis=-1, keepdims=True), l_ref.shape)
        acc_ref[...] = jnp.dot(p.astype(BF16), _pad_rows(vn_ref[0], 128).astype(BF16), preferred_element_type=F32)

    picked = jnp.max(jnp.where(lane == n, sel_ref[...], 0.0), axis=-1, keepdims=True) > 0.5
    lg = jnp.where(picked, lg_ref[0, 0], NEG)
    m_old = m_ref[:, :1]
    m_new = jnp.maximum(m_old, jnp.max(lg, axis=-1, keepdims=True))
    alpha = jnp.exp(m_old - m_new)
    p = jnp.exp(lg - m_new)
    m_ref[...] = jnp.broadcast_to(m_new, m_ref.shape)
    l_ref[...] = alpha * l_ref[...] + jnp.sum(p, axis=-1, keepdims=True)
    pb = p.astype(BF16)
    acc_ref[...] = (alpha * acc_ref[...]
                    + jnp.dot(pb[:, :PAGE_SIZE], v0_ref[0].astype(BF16), preferred_element_type=F32)
                    + jnp.dot(pb[:, PAGE_SIZE:], v1_ref[0].astype(BF16), preferred_element_type=F32))

    @pl.when(n == pl.num_programs(1) - 1)
    def _():
        out = acc_ref[...] / l_ref[:, :1]
        for h in range(ATTN_HEADS):
            o_ref[0, :, h * ATTN_DH:(h + 1) * ATTN_DH] = (
                out[h * n_new:(h + 1) * n_new, h * ATTN_DH:(h + 1) * ATTN_DH].astype(o_ref.dtype))


def moba_sample(q, k_new, v_new, cache_k, cache_v, page_table):
    b, l, d = q.shape
    n_pages = PAST_LEN // PAGE_SIZE
    assert n_pages % PAGES_PER_BLOCK == 0 and (PAST_LEN + l - 1) // MOBA_BLOCK == PAST_LEN // MOBA_BLOCK
    nblk = n_pages // PAGES_PER_BLOCK
    rows = ATTN_HEADS * l
    n_phys = cache_k.shape[0]
    ck = cache_k.reshape(n_phys, PAGE_SIZE, d)
    cv = cache_v.reshape(n_phys, PAGE_SIZE, d)
    wq = _block_diag_queries(q.reshape(b, l, ATTN_HEADS, ATTN_DH))
    wqs = (wq * ATTN_DH ** -0.5).astype(BF16)

    per_batch = lambda r, c: pl.BlockSpec((1, r, c), lambda bi, n, pt: (bi, 0, 0))
    page = lambda j: pl.BlockSpec((1, PAGE_SIZE, d), lambda bi, n, pt: (pt[bi, PAGES_PER_BLOCK * n + j], 0, 0))
    logit_spec = pl.BlockSpec((1, 1, rows, MOBA_BLOCK), lambda bi, n, pt: (bi, n, 0, 0))

    bsum, logits = pl.pallas_call(
        _moba_sample_keys_kernel,
        grid_spec=pltpu.PrefetchScalarGridSpec(
            num_scalar_prefetch=1,
            grid=(b, nblk),
            in_specs=[per_batch(rows, d), page(0), page(1)],
            out_specs=[pl.BlockSpec((1, 1, 1, d), lambda bi, n, pt: (bi, n, 0, 0)), logit_spec]),
        out_shape=[jax.ShapeDtypeStruct((b, nblk, 1, d), F32),
                   jax.ShapeDtypeStruct((b, nblk, rows, MOBA_BLOCK), F32)],
        compiler_params=_params("parallel", "arbitrary"),
        name="moba_sample_keys",
    )(page_table, wqs, ck, ck)

    return pl.pallas_call(
        functools.partial(_moba_sample_values_kernel, n_new=l, n_past_blocks=nblk),
        grid_spec=pltpu.PrefetchScalarGridSpec(
            num_scalar_prefetch=1,
            grid=(b, nblk),
            in_specs=[per_batch(rows, d), per_batch(rows, d), per_batch(nblk, d), per_batch(l, d), per_batch(l, d),
                      logit_spec, page(0), page(1)],
            out_specs=per_batch(l, d),
            scratch_shapes=[pltpu.VMEM((rows, 128), F32), pltpu.VMEM((rows, 128), F32),
                            pltpu.VMEM((rows, 128), F32), pltpu.VMEM((rows, d), F32)]),
        out_shape=jax.ShapeDtypeStruct((b, l, d), F32),
        compiler_params=_params("parallel", "arbitrary"),
        name="moba_sample_values",
    )(page_table, wq, wqs, bsum.reshape(b, nblk, d), k_new, v_new, logits, cv, cv)


def kernel(x_prompt, x_sample, state_ret, cache_k, cache_v, page_table, ret_norm_g, ret_w_in, ret_w_out,
           attn_norm_g, attn_w_q, attn_w_out, kv_norm_g, w_kv, mlp_norm_g, mlp_w_up, mlp_w_down, final_norm_g):
    bp, lp, d = x_prompt.shape
    bs, ls, _ = x_sample.shape
    assert ret_w_in.shape[0] == 1 and attn_w_q.shape[0] == 1 and mlp_w_up.shape[0] == 2
    w_in = ret_w_in[0].astype(BF16)
    w_out = ret_w_out[0].astype(BF16)
    w_q = attn_w_q[0].astype(BF16)
    w_o = attn_w_out[0].astype(BF16)
    w_k = w_kv[:, :d].astype(BF16)
    w_v = w_kv[:, d:].astype(BF16)
    w_up = mlp_w_up.astype(BF16)
    w_down = mlp_w_down.astype(BF16)

    def trunk(x, pos, s0, attend):
        b, l, _ = x.shape
        h = x.reshape(b * l, d)
        act = BF16 if l % RET_ROWS == 0 else F32
        proj = norm_matmul(h, ret_norm_g[0], w_in, act)
        o, s = retention(proj.reshape(b, l, -1), pos, s0)
        h = matmul_residual(o.reshape(b * l, RET_V).astype(BF16), w_out, h)
        h = mlp(h, mlp_norm_g[0], w_up[0], w_down[0])
        k, v, q = kvq_proj(h, kv_norm_g, attn_norm_g[0], w_k, w_v, w_q)
        o = attend(q.reshape(b, l, d), k.reshape(b, l, d), v.reshape(b, l, d))
        h = matmul_residual(o.reshape(b * l, d).astype(BF16), w_o, h)
        y = mlp(h, mlp_norm_g[1], w_up[1], w_down[1], g_final=final_norm_g)
        kv_shape = (b, l, ATTN_HEADS, ATTN_DH)
        return y.reshape(b, l, d), s[None], k.reshape(kv_shape), v.reshape(kv_shape)

    y_p, s_p, k_p, v_p = trunk(x_prompt, jnp.arange(lp, dtype=jnp.int32), None, moba_prompt)
    y_s, s_s, k_s, v_s = trunk(
        x_sample, PAST_LEN + jnp.arange(ls, dtype=jnp.int32), state_ret[0],
        lambda q, k, v: moba_sample(q, k, v, cache_k, cache_v, page_table))
    return (y_p, y_s, s_p, s_s, k_p, v_p, k_s, v_s)
```

```python
import functools
import math

import jax
import jax.numpy as jnp
from jax import lax
from jax.experimental import pallas as pl
from jax.experimental.pallas import tpu as pltpu

F32 = jnp.float32
BF16 = jnp.bfloat16

D_MODEL = 1024
PAST_LEN = 8192
PAGE_SIZE = 128
RET_HEADS = 4
RET_DK = D_MODEL // RET_HEADS
RET_QK = RET_HEADS * RET_DK
RET_V = 2 * D_MODEL
RET_DV = RET_V // RET_HEADS
RET_CHUNK = 128
ATTN_HEADS = 8
ATTN_DH = D_MODEL // ATTN_HEADS
MOBA_BLOCK = 256
MOBA_TOPK = 3
D_FF = 4 * D_MODEL
NORM_EPS = 1e-5
GN_EPS = 1e-6
NEG = -1e30

PAGES_PER_BLOCK = MOBA_BLOCK // PAGE_SIZE
STEP_PAGES = 8
STEP_BLOCKS = STEP_PAGES // PAGES_PER_BLOCK
RET_ROWS = 128
VMEM_LIMIT_BYTES = 56 * 1024 * 1024

_NT = (((1,), (1,)), ((), ()))
_TN = (((0,), (0,)), ((), ()))


def _params(*sem):
    return pltpu.CompilerParams(dimension_semantics=sem, vmem_limit_bytes=VMEM_LIMIT_BYTES)


def _rms(x, g, eps=NORM_EPS):
    return x * lax.rsqrt(jnp.mean(x * x, axis=-1, keepdims=True) + eps) * g


def _norm_matmul_kernel(x_ref, g_ref, w_ref, o_ref, xn_ref):
    @pl.when(pl.program_id(1) == 0)
    def _():
        xn_ref[...] = _rms(x_ref[...], g_ref[...]).astype(BF16)

    o_ref[...] = jnp.dot(xn_ref[...], w_ref[...], preferred_element_type=F32).astype(o_ref.dtype)


def norm_matmul(x, g, w, out_dtype, tm=1024, tn=1024):
    m, d = x.shape
    n = w.shape[1]
    tm = min(tm, m)
    return pl.pallas_call(
        _norm_matmul_kernel,
        grid=(m // tm, n // tn),
        in_specs=[pl.BlockSpec((tm, d), lambda i, j: (i, 0)),
                  pl.BlockSpec((1, d), lambda i, j: (0, 0)),
                  pl.BlockSpec((d, tn), lambda i, j: (0, j))],
        out_specs=pl.BlockSpec((tm, tn), lambda i, j: (i, j)),
        out_shape=jax.ShapeDtypeStruct((m, n), out_dtype),
        scratch_shapes=[pltpu.VMEM((tm, d), BF16)],
        compiler_params=_params("parallel", "arbitrary"),
        name="norm_matmul",
    )(x, g.reshape(1, d), w)


def _matmul_residual_kernel(a_ref, w_ref, r_ref, o_ref):
    o_ref[...] = r_ref[...] + jnp.dot(a_ref[...], w_ref[...], preferred_element_type=F32)


def matmul_residual(a, w, res, tm=512):
    m, k = a.shape
    n = w.shape[1]
    tm = min(tm, m)
    return pl.pallas_call(
        _matmul_residual_kernel,
        grid=(m // tm,),
        in_specs=[pl.BlockSpec((tm, k), lambda i: (i, 0)),
                  pl.BlockSpec((k, n), lambda i: (0, 0)),
                  pl.BlockSpec((tm, n), lambda i: (i, 0))],
        out_specs=pl.BlockSpec((tm, n), lambda i: (i, 0)),
        out_shape=jax.ShapeDtypeStruct((m, n), F32),
        compiler_params=_params("parallel"),
        name="matmul_residual",
    )(a, w, res)


def _mlp_kernel(x_ref, g_ref, wu_ref, wd_ref, gf_ref, o_ref, xn_ref, acc_ref, *, final_norm):
    f = pl.program_id(1)

    @pl.when(f == 0)
    def _():
        xn_ref[...] = _rms(x_ref[...], g_ref[...]).astype(BF16)
        acc_ref[...] = jnp.zeros_like(acc_ref)

    u = jnp.maximum(jnp.dot(xn_ref[...], wu_ref[...], preferred_element_type=F32), 0.0)
    acc_ref[...] += jnp.dot((u * u).astype(BF16), wd_ref[...], preferred_element_type=F32)

    @pl.when(f == pl.num_programs(1) - 1)
    def _():
        h = x_ref[...] + acc_ref[...]
        o_ref[...] = _rms(h, gf_ref[...]) if final_norm else h


def mlp(x, g, w_up, w_down, g_final=None, tm=512, tf=1024):
    m, d = x.shape
    ff = w_up.shape[1]
    tm = min(tm, m)
    final_norm = g_final is not None
    gf = g_final if final_norm else g
    return pl.pallas_call(
        functools.partial(_mlp_kernel, final_norm=final_norm),
        grid=(m // tm, ff // tf),
        in_specs=[pl.BlockSpec((tm, d), lambda i, f: (i, 0)),
                  pl.BlockSpec((1, d), lambda i, f: (0, 0)),
                  pl.BlockSpec((d, tf), lambda i, f: (0, f)),
                  pl.BlockSpec((tf, d), lambda i, f: (f, 0)),
                  pl.BlockSpec((1, d), lambda i, f: (0, 0))],
        out_specs=pl.BlockSpec((tm, d), lambda i, f: (i, 0)),
        out_shape=jax.ShapeDtypeStruct((m, d), F32),
        scratch_shapes=[pltpu.VMEM((tm, d), BF16), pltpu.VMEM((tm, d), F32)],
        compiler_params=_params("parallel", "arbitrary"),
        name="mlp",
    )(x, g.reshape(1, d), w_up, w_down, gf.reshape(1, d))


def _kvq_kernel(x_ref, gkv_ref, gq_ref, wk_ref, wv_ref, wq_ref, k_ref, v_ref, q_ref):
    x = x_ref[...]
    r = x * lax.rsqrt(jnp.mean(x * x, axis=-1, keepdims=True) + NORM_EPS)
    xkv = (r * gkv_ref[...]).astype(BF16)
    xq = (r * gq_ref[...]).astype(BF16)
    k_ref[...] = jnp.dot(xkv, wk_ref[...], preferred_element_type=F32)
    v_ref[...] = jnp.dot(xkv, wv_ref[...], preferred_element_type=F32)
    q_ref[...] = jnp.dot(xq, wq_ref[...], preferred_element_type=F32)


def kvq_proj(x, g_kv, g_q, wk, wv, wq, tm=512):
    m, d = x.shape
    tm = min(tm, m)
    row = pl.BlockSpec((tm, d), lambda i: (i, 0))
    gain = pl.BlockSpec((1, d), lambda i: (0, 0))
    weight = pl.BlockSpec((d, d), lambda i: (0, 0))
    out = jax.ShapeDtypeStruct((m, d), F32)
    return pl.pallas_call(
        _kvq_kernel,
        grid=(m // tm,),
        in_specs=[row, gain, gain, weight, weight, weight],
        out_specs=[row, row, row],
        out_shape=[out, out, out],
        compiler_params=_params("parallel"),
        name="kvq_proj",
    )(x, g_kv.reshape(1, d), g_q.reshape(1, d), wk, wv, wq)


def _ret_log_decay():
    return jnp.log(1.0 - 2.0 ** (-5.0 - jnp.arange(RET_HEADS, dtype=F32)))


def _ret_tables(pos, chunk):
    angle = 1.0 / (10000.0 ** jnp.linspace(0.0, 1.0, RET_DK // 2, dtype=F32))
    angle = jnp.repeat(angle, 2)
    ang = pos.astype(F32)[:, None] * angle[None, :]
    sin, cos = jnp.sin(ang), jnp.cos(ang)
    even = (jnp.arange(RET_DK) % 2 == 0)[None, :]
    sin_next = jnp.where(even, -sin, 0.0)
    sin_prev = jnp.where(even, 0.0, sin)

    lg = _ret_log_decay()
    i = jnp.arange(chunk, dtype=F32)
    diff = i[:, None] - i[None, :]
    d_intra = jnp.exp(jnp.where(diff >= 0, lg[:, None, None] * diff, -jnp.inf))
    q_dec = jnp.exp(lg[:, None] * (i + 1.0))
    k_dec = jnp.exp(lg[:, None] * (chunk - 1.0 - i))
    c_dec = jnp.exp(lg * chunk)
    pad = RET_ROWS - chunk
    d_intra = jnp.pad(d_intra, ((0, 0), (0, pad), (0, pad)))
    q_dec = jnp.broadcast_to(jnp.pad(q_dec, ((0, 0), (0, pad)))[:, :, None], (RET_HEADS, RET_ROWS, RET_DK))
    k_dec = jnp.broadcast_to(jnp.pad(k_dec, ((0, 0), (0, pad)))[:, :, None], (RET_HEADS, RET_ROWS, RET_DK))
    c_dec = jnp.broadcast_to(c_dec[:, None, None], (RET_HEADS, 1, RET_DV))
    return cos, sin_next, sin_prev, d_intra, q_dec, k_dec, c_dec


def _theta_shift(x, cos, sin_next, sin_prev):
    dk = x.shape[-1]
    return x * cos + pltpu.roll(x, dk - 1, 1) * sin_next + pltpu.roll(x, 1, 1) * sin_prev


def _pad_rows(x, rows):
    if x.shape[0] == rows:
        return x
    return jnp.concatenate([x, jnp.zeros((rows - x.shape[0], x.shape[1]), x.dtype)], axis=0)


def _retention_kernel(q_ref, k_ref, v_ref, gate_ref, cos_ref, sn_ref, sp_ref, dintra_ref, qdec_ref, kdec_ref,
                      cdec_ref, s0_ref, o_ref, sout_ref, s_ref, *, chunk, has_state):
    c = pl.program_id(1)

    @pl.when(c == 0)
    def _():
        s_ref[...] = s0_ref[0] if has_state else jnp.zeros_like(s_ref)

    cos, sn, sp = cos_ref[...], sn_ref[...], sp_ref[...]
    for h in range(RET_HEADS):
        qk_cols = slice(h * RET_DK, (h + 1) * RET_DK)
        v_cols = slice(h * RET_DV, (h + 1) * RET_DV)
        q = _theta_shift(q_ref[0, :, qk_cols].astype(F32), cos, sn, sp)
        k = _theta_shift(k_ref[0, :, qk_cols].astype(F32), cos, sn, sp) * (RET_DK ** -0.5)
        q = _pad_rows(q, RET_ROWS)
        k = _pad_rows(k, RET_ROWS)
        v = _pad_rows(v_ref[0, :, v_cols], RET_ROWS).astype(BF16)
        s = s_ref[h]

        a = lax.dot_general(q.astype(BF16), k.astype(BF16), _NT, preferred_element_type=F32) * dintra_ref[h]
        o = (jnp.dot(a.astype(BF16), v, preferred_element_type=F32)
             + jnp.dot((q * qdec_ref[h]).astype(BF16), s.astype(BF16), preferred_element_type=F32))
        s_ref[h] = s * cdec_ref[h] + lax.dot_general((k * kdec_ref[h]).astype(BF16), v, _TN,
                                                     preferred_element_type=F32)

        o = o[:chunk]
        o = o * lax.rsqrt(jnp.mean(o * o, axis=-1, keepdims=True) + GN_EPS)
        gate = gate_ref[0, :, v_cols].astype(F32)
        o_ref[0, :, v_cols] = (o * (gate * jax.nn.sigmoid(gate))).astype(o_ref.dtype)

    @pl.when(c == pl.num_programs(1) - 1)
    def _():
        sout_ref[0] = s_ref[...]


def retention(proj, pos, s0):
    b, l, _ = proj.shape
    chunk = math.gcd(l, RET_CHUNK)
    n = l // chunk
    has_state = s0 is not None
    if not has_state:
        s0 = jnp.zeros((1, RET_HEADS, RET_DK, RET_DV), F32)
    tables = _ret_tables(pos, chunk)
    k_col, v_col, gate_col = 1, 2 * RET_QK // RET_V, 2 * RET_QK // RET_V + 1
    pos_tab = pl.BlockSpec((chunk, RET_DK), lambda bi, c: (c, 0))
    head_tab = lambda rows, cols: pl.BlockSpec((RET_HEADS, rows, cols), lambda bi, c: (0, 0, 0))
    state = lambda index_map: pl.BlockSpec((1, RET_HEADS, RET_DK, RET_DV), index_map)
    return pl.pallas_call(
        functools.partial(_retention_kernel, chunk=chunk, has_state=has_state),
        grid=(b, n),
        in_specs=[pl.BlockSpec((1, chunk, RET_QK), lambda bi, c: (bi, c, 0)),
                  pl.BlockSpec((1, chunk, RET_QK), lambda bi, c: (bi, c, k_col)),
                  pl.BlockSpec((1, chunk, RET_V), lambda bi, c: (bi, c, v_col)),
                  pl.BlockSpec((1, chunk, RET_V), lambda bi, c: (bi, c, gate_col)),
                  pos_tab, pos_tab, pos_tab,
                  head_tab(RET_ROWS, RET_ROWS), head_tab(RET_ROWS, RET_DK), head_tab(RET_ROWS, RET_DK),
                  head_tab(1, RET_DV),
                  state((lambda bi, c: (bi, 0, 0, 0)) if has_state else (lambda bi, c: (0, 0, 0, 0)))],
        out_specs=[pl.BlockSpec((1, chunk, RET_V), lambda bi, c: (bi, c, 0)),
                   state(lambda bi, c: (bi, 0, 0, 0))],
        out_shape=[jax.ShapeDtypeStruct((b, l, RET_V), proj.dtype),
                   jax.ShapeDtypeStruct((b, RET_HEADS, RET_DK, RET_DV), F32)],
        scratch_shapes=[pltpu.VMEM((RET_HEADS, RET_DK, RET_DV), F32)],
        compiler_params=_params("parallel", "arbitrary"),
        name="retention",
    )(proj, proj, proj, proj, *tables, s0)


def _moba_prompt_kernel(q_ref, k_ref, v_ref, o_ref, *, nb):
    k = k_ref[0]
    kb = k.astype(BF16)
    vb = v_ref[0].astype(BF16)
    means = jnp.sum(k.reshape(nb, MOBA_BLOCK, ATTN_DH), axis=1) * (1.0 / MOBA_BLOCK)
    means = _pad_rows(means, 128)
    lane = lax.broadcasted_iota(jnp.int32, (MOBA_BLOCK, 128), 1)
    row = lax.broadcasted_iota(jnp.int32, (MOBA_BLOCK, MOBA_BLOCK), 0)
    col = lax.broadcasted_iota(jnp.int32, (MOBA_BLOCK, MOBA_BLOCK), 1)
    causal = col <= row

    for i in range(nb):
        q = q_ref[0, i * MOBA_BLOCK:(i + 1) * MOBA_BLOCK, :]
        qb = (q * ATTN_DH ** -0.5).astype(BF16)
        kv_rows = (i + 1) * MOBA_BLOCK
        logits = lax.dot_general(qb, kb[:kv_rows], _NT, preferred_element_type=F32)
        pieces = [jnp.where(causal, logits[:, i * MOBA_BLOCK:], NEG)]
        if i > 0:
            s = lax.dot_general(q, means, _NT, preferred_element_type=F32, precision=lax.Precision.HIGHEST)
            elig = lane < i
            sm = jnp.where(elig, s, NEG)
            rank = jnp.zeros(sm.shape, F32)
            for m in range(i):
                cm = sm[:, m:m + 1]
                beats = jnp.where(cm > sm, 1.0, jnp.where(cm == sm, jnp.where(lane > m, 1.0, 0.0), 0.0))
                rank = rank + beats
            sel = jnp.where(elig, jnp.where(rank < MOBA_TOPK, jnp.where(sm > 0.5 * NEG, 1.0, 0.0), 0.0), 0.0)
            for n in range(i):
                pieces.append(jnp.where(sel[:, n:n + 1] > 0.5, logits[:, n * MOBA_BLOCK:(n + 1) * MOBA_BLOCK], NEG))
        mx = functools.reduce(jnp.maximum, [jnp.max(p, axis=-1, keepdims=True) for p in pieces])
        probs = [jnp.exp(p - mx) for p in pieces]
        denom = functools.reduce(lambda x, y: x + y, [jnp.sum(p, axis=-1, keepdims=True) for p in probs])
        blocks = [i] + list(range(i))
        acc = functools.reduce(lambda x, y: x + y, [
            jnp.dot(p.astype(BF16), vb[n * MOBA_BLOCK:(n + 1) * MOBA_BLOCK], preferred_element_type=F32)
            for p, n in zip(probs, blocks)])
        o_ref[0, i * MOBA_BLOCK:(i + 1) * MOBA_BLOCK, :] = (acc / denom).astype(o_ref.dtype)


def moba_prompt(q, k, v):
    b, s, _ = q.shape
    assert s % MOBA_BLOCK == 0
    nb = s // MOBA_BLOCK
    spec = pl.BlockSpec((1, s, ATTN_DH), lambda bi, h: (bi, 0, h))
    return pl.pallas_call(
        functools.partial(_moba_prompt_kernel, nb=nb),
        grid=(b, ATTN_HEADS),
        in_specs=[spec, spec, spec],
        out_specs=spec,
        out_shape=jax.ShapeDtypeStruct((b, s, ATTN_HEADS * ATTN_DH), BF16),
        compiler_params=_params("parallel", "parallel"),
        name="moba_prompt",
    )(q, k, v)


def _block_diag_queries(q):
    b, l, h, dh = q.shape
    eye = jnp.eye(h, dtype=q.dtype)
    w = q.transpose(0, 2, 1, 3)[:, :, :, None, :] * eye[None, :, None, :, None]
    return w.reshape(b, h * l, h * dh)


def _page_rows(page_ref):
    heads = [page_ref[pl.ds(h, PAGE_SIZE, stride=ATTN_HEADS), :] for h in range(ATTN_HEADS)]
    return jnp.concatenate(heads, axis=1).astype(BF16)


def _moba_sample_keys_kernel(pt_ref, wq_ref, *refs):
    del pt_ref
    k_refs, (bsum_ref, lg_ref) = refs[:STEP_PAGES], refs[STEP_PAGES:]
    sums = [jnp.sum(r[...].reshape(PAGE_SIZE, ATTN_HEADS, ATTN_DH), axis=0) for r in k_refs]
    for j in range(STEP_BLOCKS):
        bsum_ref[0, j] = functools.reduce(lambda x, y: x + y, sums[j * PAGES_PER_BLOCK:(j + 1) * PAGES_PER_BLOCK])
    keys = jnp.concatenate([_page_rows(r) for r in k_refs], axis=0)
    logits = lax.dot_general(wq_ref[0], keys, _NT, preferred_element_type=F32)
    for j in range(STEP_BLOCKS):
        lg_ref[0, j] = logits[:, j * MOBA_BLOCK:(j + 1) * MOBA_BLOCK]


def _moba_sample_values_kernel(pt_ref, wq_ref, wqs_ref, bsum_ref, kn_ref, vn_ref, lg_ref, *refs,
                               n_new, n_past_blocks):
    del pt_ref
    v_refs, (o_ref, sel_ref, m_ref, l_ref, acc_ref) = refs[:STEP_PAGES], refs[STEP_PAGES:]
    n = pl.program_id(1)
    rows = wq_ref.shape[1]
    lane = lax.broadcasted_iota(jnp.int32, (rows, 128), 1)

    @pl.when(n == 0)
    def _():
        kn = kn_ref[0]
        first_row = lax.broadcasted_iota(jnp.int32, (8, kn.shape[1]), 0) == 0
        own_sum = jnp.where(first_row, jnp.sum(kn, axis=0, keepdims=True), 0.0)
        means = _pad_rows(jnp.concatenate([bsum_ref[0], own_sum], axis=0) * (1.0 / MOBA_BLOCK), 128)
        s = lax.dot_general(wq_ref[0], means, _NT, preferred_element_type=F32, precision=lax.Precision.HIGHEST)
        work = jnp.where(lane < n_past_blocks, s, NEG)
        sel = jnp.zeros(work.shape, F32)
        for _ in range(MOBA_TOPK):
            mx = jnp.max(work, axis=-1, keepdims=True)
            first = jnp.min(jnp.where(work == mx, lane, 128), axis=-1, keepdims=True)
            pick = lane == first
            sel = jnp.where(pick, jnp.where(mx > 0.5 * NEG, 1.0, 0.0), sel)
            work = jnp.where(pick, -jnp.inf, work)
        sel_ref[...] = sel
        t = lax.broadcasted_iota(jnp.int32, (rows, 128), 0) % n_new
        lo = lax.dot_general(wqs_ref[0], _pad_rows(kn, 128).astype(BF16), _NT, preferred_element_type=F32)
        lo = jnp.where(lane <= t, lo, NEG)
        m0 = jnp.max(lo, axis=-1, keepdims=True)
        p = jnp.exp(lo - m0)
        m_ref[...] = jnp.broadcast_to(m0, m_ref.shape)
        l_ref[...] = jnp.broadcast_to(jnp.sum(p, axis=-1, keepdims=True), l_ref.shape)
        acc_ref[...] = jnp.dot(p.astype(BF16), _pad_rows(vn_ref[0], 128).astype(BF16), preferred_element_type=F32)

    sel = sel_ref[...]
    lgs = []
    for j in range(STEP_BLOCKS):
        picked = jnp.max(jnp.where(lane == n * STEP_BLOCKS + j, sel, 0.0), axis=-1, keepdims=True) > 0.5
        lgs.append(jnp.where(picked, lg_ref[0, j], NEG))
    m_old = m_ref[:, :1]
    m_new = functools.reduce(jnp.maximum, [m_old] + [jnp.max(lg, axis=-1, keepdims=True) for lg in lgs])
    alpha = jnp.exp(m_old - m_new)
    ps = [jnp.exp(lg - m_new) for lg in lgs]
    m_ref[...] = jnp.broadcast_to(m_new, m_ref.shape)
    l_ref[...] = alpha * l_ref[...] + functools.reduce(
        lambda x, y: x + y, [jnp.sum(p, axis=-1, keepdims=True) for p in ps])
    values = jnp.concatenate([_page_rows(r) for r in v_refs], axis=0)
    acc_ref[...] = alpha * acc_ref[...] + jnp.dot(
        jnp.concatenate(ps, axis=1).astype(BF16), values, preferred_element_type=F32)

    @pl.when(n == pl.num_programs(1) - 1)
    def _():
        out = acc_ref[...] / l_ref[:, :1]
        for h in range(ATTN_HEADS):
            o_ref[0, :, h * ATTN_DH:(h + 1) * ATTN_DH] = (
                out[h * n_new:(h + 1) * n_new, h * ATTN_DH:(h + 1) * ATTN_DH].astype(o_ref.dtype))


def moba_sample(q, k_new, v_new, cache_k, cache_v, page_table):
    b, l, d = q.shape
    n_pages = PAST_LEN // PAGE_SIZE
    assert n_pages % STEP_PAGES == 0 and (PAST_LEN + l - 1) // MOBA_BLOCK == PAST_LEN // MOBA_BLOCK
    nblk = n_pages // PAGES_PER_BLOCK
    steps = n_pages // STEP_PAGES
    rows = ATTN_HEADS * l
    page_rows = PAGE_SIZE * ATTN_HEADS
    ck = cache_k.reshape(-1, ATTN_DH)
    cv = cache_v.reshape(-1, ATTN_DH)
    wq = _block_diag_queries(q.reshape(b, l, ATTN_HEADS, ATTN_DH))
    wqs = (wq * ATTN_DH ** -0.5).astype(BF16)

    per_batch = lambda r, c: pl.BlockSpec((1, r, c), lambda bi, n, pt: (bi, 0, 0))
    pages = [pl.BlockSpec((page_rows, ATTN_DH), functools.partial(
        lambda bi, n, pt, j: (pt[bi, STEP_PAGES * n + j], 0), j=j)) for j in range(STEP_PAGES)]
    logit_spec = pl.BlockSpec((1, STEP_BLOCKS, rows, MOBA_BLOCK), lambda bi, n, pt: (bi, n, 0, 0))

    bsum, logits = pl.pallas_call(
        _moba_sample_keys_kernel,
        grid_spec=pltpu.PrefetchScalarGridSpec(
            num_scalar_prefetch=1,
            grid=(b, steps),
            in_specs=[per_batch(rows, d)] + pages,
            out_specs=[pl.BlockSpec((1, STEP_BLOCKS, ATTN_HEADS, ATTN_DH), lambda bi, n, pt: (bi, n, 0, 0)),
                       logit_spec]),
        out_shape=[jax.ShapeDtypeStruct((b, nblk, ATTN_HEADS, ATTN_DH), F32),
                   jax.ShapeDtypeStruct((b, nblk, rows, MOBA_BLOCK), F32)],
        compiler_params=_params("parallel", "arbitrary"),
        name="moba_sample_keys",
    )(page_table, wqs, *([ck] * STEP_PAGES))

    return pl.pallas_call(
        functools.partial(_moba_sample_values_kernel, n_new=l, n_past_blocks=nblk),
        grid_spec=pltpu.PrefetchScalarGridSpec(
            num_scalar_prefetch=1,
            grid=(b, steps),
            in_specs=[per_batch(rows, d), per_batch(rows, d), per_batch(nblk, d), per_batch(l, d), per_batch(l, d),
                      logit_spec] + pages,
            out_specs=per_batch(l, d),
            scratch_shapes=[pltpu.VMEM((rows, 128), F32), pltpu.VMEM((rows, 128), F32),
                            pltpu.VMEM((rows, 128), F32), pltpu.VMEM((rows, d), F32)]),
        out_shape=jax.ShapeDtypeStruct((b, l, d), F32),
        compiler_params=_params("parallel", "arbitrary"),
        name="moba_sample_values",
    )(page_table, wq, wqs, bsum.reshape(b, nblk, d), k_new, v_new, logits, *([cv] * STEP_PAGES))


def kernel(x_prompt, x_sample, state_ret, cache_k, cache_v, page_table, ret_norm_g, ret_w_in, ret_w_out,
           attn_norm_g, attn_w_q, attn_w_out, kv_norm_g, w_kv, mlp_norm_g, mlp_w_up, mlp_w_down, final_norm_g):
    bp, lp, d = x_prompt.shape
    bs, ls, _ = x_sample.shape
    assert ret_w_in.shape[0] == 1 and attn_w_q.shape[0] == 1 and mlp_w_up.shape[0] == 2
    w_in = ret_w_in[0].astype(BF16)
    w_out = ret_w_out[0].astype(BF16)
    w_q = attn_w_q[0].astype(BF16)
    w_o = attn_w_out[0].astype(BF16)
    w_k = w_kv[:, :d].astype(BF16)
    w_v = w_kv[:, d:].astype(BF16)
    w_up = mlp_w_up.astype(BF16)
    w_down = mlp_w_down.astype(BF16)

    def trunk(x, pos, s0, attend):
        b, l, _ = x.shape
        h = x.reshape(b * l, d)
        act = BF16 if l % RET_ROWS == 0 else F32
        proj = norm_matmul(h, ret_norm_g[0], w_in, act)
        o, s = retention(proj.reshape(b, l, -1), pos, s0)
        h = matmul_residual(o.reshape(b * l, RET_V).astype(BF16), w_out, h)
        h = mlp(h, mlp_norm_g[0], w_up[0], w_down[0])
        k, v, q = kvq_proj(h, kv_norm_g, attn_norm_g[0], w_k, w_v, w_q)
        o = attend(q.reshape(b, l, d), k.reshape(b, l, d), v.reshape(b, l, d))
        h = matmul_residual(o.reshape(b * l, d).astype(BF16), w_o, h)
        y = mlp(h, mlp_norm_g[1], w_up[1], w_down[1], g_final=final_norm_g)
        kv_shape = (b, l, ATTN_HEADS, ATTN_DH)
        return y.reshape(b, l, d), s[None], k.reshape(kv_shape), v.reshape(kv_shape)

    y_p, s_p, k_p, v_p = trunk(x_prompt, jnp.arange(lp, dtype=jnp.int32), None, moba_prompt)
    y_s, s_s, k_s, v_s = trunk(
        x_sample, PAST_LEN + jnp.arange(ls, dtype=jnp.int32), state_ret[0],
        lambda q, k, v: moba_sample(q, k, v, cache_k, cache_v, page_table))
    return (y_p, y_s, s_p, s_s, k_p, v_p, k_s, v_s)
```

```python
import functools
import math

import jax
import jax.numpy as jnp
from jax import lax
from jax.experimental import pallas as pl
from jax.experimental.pallas import tpu as pltpu

F32 = jnp.float32
BF16 = jnp.bfloat16

D_MODEL = 1024
PAST_LEN = 8192
PAGE_SIZE = 128
RET_HEADS = 4
RET_DK = D_MODEL // RET_HEADS
RET_QK = RET_HEADS * RET_DK
RET_V = 2 * D_MODEL
RET_DV = RET_V // RET_HEADS
RET_CHUNK = 128
ATTN_HEADS = 8
ATTN_DH = D_MODEL // ATTN_HEADS
MOBA_BLOCK = 256
MOBA_TOPK = 3
D_FF = 4 * D_MODEL
NORM_EPS = 1e-5
GN_EPS = 1e-6
NEG = -1e30
LOG2_E = math.log2(math.e)

PAGES_PER_BLOCK = MOBA_BLOCK // PAGE_SIZE
STEP_PAGES = 16
STEP_BLOCKS = STEP_PAGES // PAGES_PER_BLOCK
RET_ROWS = 128
VMEM_LIMIT_BYTES = 56 * 1024 * 1024

_NT = (((1,), (1,)), ((), ()))
_TN = (((0,), (0,)), ((), ()))


def _params(*sem):
    return pltpu.CompilerParams(dimension_semantics=sem, vmem_limit_bytes=VMEM_LIMIT_BYTES)


def _rms(x, g, eps=NORM_EPS):
    return x * lax.rsqrt(jnp.mean(x * x, axis=-1, keepdims=True) + eps) * g


def _norm_matmul_kernel(x_ref, g_ref, w_ref, o_ref, xn_ref):
    @pl.when(pl.program_id(1) == 0)
    def _():
        xn_ref[...] = _rms(x_ref[...], g_ref[...]).astype(BF16)

    o_ref[...] = jnp.dot(xn_ref[...], w_ref[...], preferred_element_type=F32).astype(o_ref.dtype)


def norm_matmul(x, g, w, out_dtype, tm=1024, tn=2048):
    m, d = x.shape
    n = w.shape[1]
    tm = min(tm, m)
    return pl.pallas_call(
        _norm_matmul_kernel,
        grid=(m // tm, n // tn),
        in_specs=[pl.BlockSpec((tm, d), lambda i, j: (i, 0)),
                  pl.BlockSpec((1, d), lambda i, j: (0, 0)),
                  pl.BlockSpec((d, tn), lambda i, j: (0, j))],
        out_specs=pl.BlockSpec((tm, tn), lambda i, j: (i, j)),
        out_shape=jax.ShapeDtypeStruct((m, n), out_dtype),
        scratch_shapes=[pltpu.VMEM((tm, d), BF16)],
        compiler_params=_params("parallel", "arbitrary"),
        name="norm_matmul",
    )(x, g.reshape(1, d), w)


def _matmul_residual_kernel(a_ref, w_ref, r_ref, o_ref):
    o_ref[...] = r_ref[...] + jnp.dot(a_ref[...], w_ref[...], preferred_element_type=F32)


def matmul_residual(a, w, res, tm=1024):
    m, k = a.shape
    n = w.shape[1]
    tm = min(tm, m)
    return pl.pallas_call(
        _matmul_residual_kernel,
        grid=(m // tm,),
        in_specs=[pl.BlockSpec((tm, k), lambda i: (i, 0)),
                  pl.BlockSpec((k, n), lambda i: (0, 0)),
                  pl.BlockSpec((tm, n), lambda i: (i, 0))],
        out_specs=pl.BlockSpec((tm, n), lambda i: (i, 0)),
        out_shape=jax.ShapeDtypeStruct((m, n), F32),
        compiler_params=_params("parallel"),
        name="matmul_residual",
    )(a, w, res)


def _mlp_kernel(x_ref, g_ref, wu_ref, wd_ref, gf_ref, o_ref, xn_ref, acc_ref, *, final_norm):
    f = pl.program_id(1)

    @pl.when(f == 0)
    def _():
        xn_ref[...] = _rms(x_ref[...], g_ref[...]).astype(BF16)
        acc_ref[...] = jnp.zeros_like(acc_ref)

    u = jnp.maximum(jnp.dot(xn_ref[...], wu_ref[...], preferred_element_type=F32), 0.0)
    acc_ref[...] += jnp.dot((u * u).astype(BF16), wd_ref[...], preferred_element_type=F32)

    @pl.when(f == pl.num_programs(1) - 1)
    def _():
        h = x_ref[...] + acc_ref[...]
        o_ref[...] = _rms(h, gf_ref[...]) if final_norm else h


def mlp(x, g, w_up, w_down, g_final=None, tm=1024, tf=1024):
    m, d = x.shape
    ff = w_up.shape[1]
    tm = min(tm, m)
    final_norm = g_final is not None
    gf = g_final if final_norm else g
    return pl.pallas_call(
        functools.partial(_mlp_kernel, final_norm=final_norm),
        grid=(m // tm, ff // tf),
        in_specs=[pl.BlockSpec((tm, d), lambda i, f: (i, 0)),
                  pl.BlockSpec((1, d), lambda i, f: (0, 0)),
                  pl.BlockSpec((d, tf), lambda i, f: (0, f)),
                  pl.BlockSpec((tf, d), lambda i, f: (f, 0)),
                  pl.BlockSpec((1, d), lambda i, f: (0, 0))],
        out_specs=pl.BlockSpec((tm, d), lambda i, f: (i, 0)),
        out_shape=jax.ShapeDtypeStruct((m, d), F32),
        scratch_shapes=[pltpu.VMEM((tm, d), BF16), pltpu.VMEM((tm, d), F32)],
        compiler_params=_params("parallel", "arbitrary"),
        name="mlp",
    )(x, g.reshape(1, d), w_up, w_down, gf.reshape(1, d))


def _kvq_kernel(x_ref, gkv_ref, gq_ref, wk_ref, wv_ref, wq_ref, k_ref, v_ref, q_ref):
    x = x_ref[...]
    r = x * lax.rsqrt(jnp.mean(x * x, axis=-1, keepdims=True) + NORM_EPS)
    xkv = (r * gkv_ref[...]).astype(BF16)
    xq = (r * gq_ref[...]).astype(BF16)
    k_ref[...] = jnp.dot(xkv, wk_ref[...], preferred_element_type=F32)
    v_ref[...] = jnp.dot(xkv, wv_ref[...], preferred_element_type=F32)
    q_ref[...] = jnp.dot(xq, wq_ref[...], preferred_element_type=F32)


def kvq_proj(x, g_kv, g_q, wk, wv, wq, tm=512):
    m, d = x.shape
    tm = min(tm, m)
    row = pl.BlockSpec((tm, d), lambda i: (i, 0))
    gain = pl.BlockSpec((1, d), lambda i: (0, 0))
    weight = pl.BlockSpec((d, d), lambda i: (0, 0))
    out = jax.ShapeDtypeStruct((m, d), F32)
    return pl.pallas_call(
        _kvq_kernel,
        grid=(m // tm,),
        in_specs=[row, gain, gain, weight, weight, weight],
        out_specs=[row, row, row],
        out_shape=[out, out, out],
        compiler_params=_params("parallel"),
        name="kvq_proj",
    )(x, g_kv.reshape(1, d), g_q.reshape(1, d), wk, wv, wq)


def _ret_log_decay():
    return jnp.log(1.0 - 2.0 ** (-5.0 - jnp.arange(RET_HEADS, dtype=F32)))


def _ret_tables(pos, chunk):
    angle = 1.0 / (10000.0 ** jnp.linspace(0.0, 1.0, RET_DK // 2, dtype=F32))
    angle = jnp.repeat(angle, 2)
    ang = pos.astype(F32)[:, None] * angle[None, :]
    sin, cos = jnp.sin(ang), jnp.cos(ang)
    even = (jnp.arange(RET_DK) % 2 == 0)[None, :]
    sin_next = jnp.where(even, -sin, 0.0)
    sin_prev = jnp.where(even, 0.0, sin)

    lg = _ret_log_decay()
    i = jnp.arange(chunk, dtype=F32)
    diff = i[:, None] - i[None, :]
    d_intra = jnp.exp(jnp.where(diff >= 0, lg[:, None, None] * diff, -jnp.inf))
    q_dec = jnp.exp(lg[:, None] * (i + 1.0))
    k_dec = jnp.exp(lg[:, None] * (chunk - 1.0 - i))
    c_dec = jnp.exp(lg * chunk)
    pad = RET_ROWS - chunk
    d_intra = jnp.pad(d_intra, ((0, 0), (0, pad), (0, pad)))
    q_dec = jnp.broadcast_to(jnp.pad(q_dec, ((0, 0), (0, pad)))[:, :, None], (RET_HEADS, RET_ROWS, RET_DK))
    k_dec = jnp.broadcast_to(jnp.pad(k_dec, ((0, 0), (0, pad)))[:, :, None], (RET_HEADS, RET_ROWS, RET_DK))
    c_dec = jnp.broadcast_to(c_dec[:, None, None], (RET_HEADS, 1, RET_DV))
    return cos, sin_next, sin_prev, d_intra, q_dec, k_dec, c_dec


def _theta_shift(x, cos, sin_next, sin_prev):
    dk = x.shape[-1]
    return x * cos + pltpu.roll(x, dk - 1, 1) * sin_next + pltpu.roll(x, 1, 1) * sin_prev


def _pad_rows(x, rows):
    if x.shape[0] == rows:
        return x
    return jnp.concatenate([x, jnp.zeros((rows - x.shape[0], x.shape[1]), x.dtype)], axis=0)


def _retention_kernel(q_ref, k_ref, v_ref, gate_ref, cos_ref, sn_ref, sp_ref, dintra_ref, qdec_ref, kdec_ref,
                      cdec_ref, s0_ref, o_ref, sout_ref, s_ref, *, chunk, has_state):
    c = pl.program_id(1)

    @pl.when(c == 0)
    def _():
        s_ref[...] = s0_ref[0] if has_state else jnp.zeros_like(s_ref)

    cos, sn, sp = cos_ref[...], sn_ref[...], sp_ref[...]
    for h in range(RET_HEADS):
        qk_cols = slice(h * RET_DK, (h + 1) * RET_DK)
        v_cols = slice(h * RET_DV, (h + 1) * RET_DV)
        q = _theta_shift(q_ref[0, :, qk_cols].astype(F32), cos, sn, sp)
        k = _theta_shift(k_ref[0, :, qk_cols].astype(F32), cos, sn, sp) * (RET_DK ** -0.5)
        q = _pad_rows(q, RET_ROWS)
        k = _pad_rows(k, RET_ROWS)
        v = _pad_rows(v_ref[0, :, v_cols], RET_ROWS).astype(BF16)
        s = s_ref[h]

        a = lax.dot_general(q.astype(BF16), k.astype(BF16), _NT, preferred_element_type=F32) * dintra_ref[h]
        o = (jnp.dot(a.astype(BF16), v, preferred_element_type=F32)
             + jnp.dot((q * qdec_ref[h]).astype(BF16), s.astype(BF16), preferred_element_type=F32))
        s_ref[h] = s * cdec_ref[h] + lax.dot_general((k * kdec_ref[h]).astype(BF16), v, _TN,
                                                     preferred_element_type=F32)

        o = o[:chunk]
        o = o * lax.rsqrt(jnp.mean(o * o, axis=-1, keepdims=True) + GN_EPS)
        gate = gate_ref[0, :, v_cols].astype(F32)
        o_ref[0, :, v_cols] = (o * (gate * jax.nn.sigmoid(gate))).astype(o_ref.dtype)

    @pl.when(c == pl.num_programs(1) - 1)
    def _():
        sout_ref[0] = s_ref[...]


def retention(proj, pos, s0):
    b, l, _ = proj.shape
    chunk = math.gcd(l, RET_CHUNK)
    n = l // chunk
    has_state = s0 is not None
    if not has_state:
        s0 = jnp.zeros((1, RET_HEADS, RET_DK, RET_DV), F32)
    tables = _ret_tables(pos, chunk)
    k_col, v_col, gate_col = 1, 2 * RET_QK // RET_V, 2 * RET_QK // RET_V + 1
    pos_tab = pl.BlockSpec((chunk, RET_DK), lambda bi, c: (c, 0))
    head_tab = lambda rows, cols: pl.BlockSpec((RET_HEADS, rows, cols), lambda bi, c: (0, 0, 0))
    state = lambda index_map: pl.BlockSpec((1, RET_HEADS, RET_DK, RET_DV), index_map)
    return pl.pallas_call(
        functools.partial(_retention_kernel, chunk=chunk, has_state=has_state),
        grid=(b, n),
        in_specs=[pl.BlockSpec((1, chunk, RET_QK), lambda bi, c: (bi, c, 0)),
                  pl.BlockSpec((1, chunk, RET_QK), lambda bi, c: (bi, c, k_col)),
                  pl.BlockSpec((1, chunk, RET_V), lambda bi, c: (bi, c, v_col)),
                  pl.BlockSpec((1, chunk, RET_V), lambda bi, c: (bi, c, gate_col)),
                  pos_tab, pos_tab, pos_tab,
                  head_tab(RET_ROWS, RET_ROWS), head_tab(RET_ROWS, RET_DK), head_tab(RET_ROWS, RET_DK),
                  head_tab(1, RET_DV),
                  state((lambda bi, c: (bi, 0, 0, 0)) if has_state else (lambda bi, c: (0, 0, 0, 0)))],
        out_specs=[pl.BlockSpec((1, chunk, RET_V), lambda bi, c: (bi, c, 0)),
                   state(lambda bi, c: (bi, 0, 0, 0))],
        out_shape=[jax.ShapeDtypeStruct((b, l, RET_V), proj.dtype),
                   jax.ShapeDtypeStruct((b, RET_HEADS, RET_DK, RET_DV), F32)],
        scratch_shapes=[pltpu.VMEM((RET_HEADS, RET_DK, RET_DV), F32)],
        compiler_params=_params("parallel", "arbitrary"),
        name="retention",
    )(proj, proj, proj, proj, *tables, s0)


def _moba_prompt_kernel(q_ref, k_ref, v_ref, o_ref, *, nb):
    k = k_ref[0]
    s_len = k.shape[0]
    means = jnp.sum(k.reshape(nb, MOBA_BLOCK, ATTN_DH), axis=1) * (1.0 / MOBA_BLOCK)
    means = _pad_rows(means, -(-nb // 8) * 8)
    blk = lax.broadcasted_iota(jnp.int32, (means.shape[0], MOBA_BLOCK), 0)
    key_lane = lax.broadcasted_iota(jnp.int32, (s_len, 128), 1)
    key_block = lax.broadcasted_iota(jnp.int32, (s_len, 128), 0) // MOBA_BLOCK
    kb = jnp.concatenate([k.astype(BF16), jnp.where(key_block == key_lane, 1.0, 0.0).astype(BF16)], axis=1)
    vb = jnp.concatenate([v_ref[0].astype(BF16), jnp.where(key_lane == 0, 1.0, 0.0).astype(BF16)], axis=1)
    row = lax.broadcasted_iota(jnp.int32, (MOBA_BLOCK, MOBA_BLOCK), 0)
    col = lax.broadcasted_iota(jnp.int32, (MOBA_BLOCK, MOBA_BLOCK), 1)
    causal = col <= row
    eye = jnp.where(col == row, 1.0, 0.0).astype(BF16)

    for i in range(nb):
        q = q_ref[0, i * MOBA_BLOCK:(i + 1) * MOBA_BLOCK, :]
        bias = jnp.zeros((MOBA_BLOCK, 128), BF16)
        if i > 0:
            s = lax.dot_general(means, q, _NT, preferred_element_type=F32, precision=lax.Precision.HIGHEST)
            elig = blk < i
            sm = jnp.where(elig, s, NEG)
            rank = jnp.zeros(sm.shape, F32)
            for m in range(i):
                cm = sm[m:m + 1, :]
                beats = jnp.where(cm > sm, 1.0, jnp.where(cm == sm, jnp.where(blk > m, 1.0, 0.0), 0.0))
                rank = rank + beats
            keep = jnp.where(rank < MOBA_TOPK, jnp.where(sm > 0.5 * NEG, 0.0, NEG), NEG)
            bias_t = _pad_rows(jnp.where(elig, keep, 0.0), 128).astype(BF16)
            bias = lax.dot_general(eye, bias_t, _NT, preferred_element_type=F32).astype(BF16)
        qa = jnp.concatenate([(q * (ATTN_DH ** -0.5 * LOG2_E)).astype(BF16), bias], axis=1)
        kv_rows = (i + 1) * MOBA_BLOCK
        logits = lax.dot_general(qa, kb[:kv_rows], _NT, preferred_element_type=F32)
        own = jnp.where(causal, logits[:, i * MOBA_BLOCK:], NEG)
        logits = own if i == 0 else jnp.concatenate([logits[:, :i * MOBA_BLOCK], own], axis=1)
        p = jnp.exp2(logits - jnp.max(logits, axis=-1, keepdims=True))
        acc = jnp.dot(p.astype(BF16), vb[:kv_rows], preferred_element_type=F32)
        o_ref[0, i * MOBA_BLOCK:(i + 1) * MOBA_BLOCK, :] = (
            acc[:, :ATTN_DH] / acc[:, ATTN_DH:ATTN_DH + 1]).astype(o_ref.dtype)


def moba_prompt(q, k, v):
    b, s, _ = q.shape
    assert s % MOBA_BLOCK == 0
    nb = s // MOBA_BLOCK
    spec = pl.BlockSpec((1, s, ATTN_DH), lambda bi, h: (bi, 0, h))
    return pl.pallas_call(
        functools.partial(_moba_prompt_kernel, nb=nb),
        grid=(b, ATTN_HEADS),
        in_specs=[spec, spec, spec],
        out_specs=spec,
        out_shape=jax.ShapeDtypeStruct((b, s, ATTN_HEADS * ATTN_DH), BF16),
        compiler_params=_params("parallel", "parallel"),
        name="moba_prompt",
    )(q, k, v)


def _block_diag_queries(q):
    b, l, h, dh = q.shape
    eye = jnp.eye(h, dtype=q.dtype)
    w = q.transpose(0, 2, 1, 3)[:, :, :, None, :] * eye[None, :, None, :, None]
    return w.reshape(b, h * l, h * dh)


def _page_rows(page_ref):
    heads = [page_ref[pl.ds(h, PAGE_SIZE, stride=ATTN_HEADS), :] for h in range(ATTN_HEADS)]
    return jnp.concatenate(heads, axis=1).astype(BF16)


def _moba_sample_keys_kernel(pt_ref, wq_ref, *refs):
    del pt_ref
    k_refs, (bsum_ref, lg_ref) = refs[:STEP_PAGES], refs[STEP_PAGES:]
    sums = [jnp.sum(r[...].reshape(PAGE_SIZE, ATTN_HEADS, ATTN_DH), axis=0) for r in k_refs]
    for j in range(STEP_BLOCKS):
        bsum_ref[0, j] = functools.reduce(lambda x, y: x + y, sums[j * PAGES_PER_BLOCK:(j + 1) * PAGES_PER_BLOCK])
    keys = jnp.concatenate([_page_rows(r) for r in k_refs], axis=0)
    logits = lax.dot_general(wq_ref[0], keys, _NT, preferred_element_type=F32)
    for j in range(STEP_BLOCKS):
        lg_ref[0, j] = logits[:, j * MOBA_BLOCK:(j + 1) * MOBA_BLOCK]


def _moba_sample_values_kernel(pt_ref, wq_ref, wqs_ref, bsum_ref, kn_ref, vn_ref, lg_ref, *refs,
                               n_new, n_past_blocks):
    del pt_ref
    v_refs, (o_ref, sel_ref, m_ref, l_ref, acc_ref) = refs[:STEP_PAGES], refs[STEP_PAGES:]
    n = pl.program_id(1)
    rows = wq_ref.shape[1]
    lane = lax.broadcasted_iota(jnp.int32, (rows, 128), 1)

    @pl.when(n == 0)
    def _():
        kn = kn_ref[0]
        first_row = lax.broadcasted_iota(jnp.int32, (8, kn.shape[1]), 0) == 0
        own_sum = jnp.where(first_row, jnp.sum(kn, axis=0, keepdims=True), 0.0)
        means = _pad_rows(jnp.concatenate([bsum_ref[0], own_sum], axis=0) * (1.0 / MOBA_BLOCK), 128)
        s = lax.dot_general(wq_ref[0], means, _NT, preferred_element_type=F32, precision=lax.Precision.HIGHEST)
        work = jnp.where(lane < n_past_blocks, s, NEG)
        sel = jnp.zeros(work.shape, F32)
        for _ in range(MOBA_TOPK):
            mx = jnp.max(work, axis=-1, keepdims=True)
            first = jnp.min(jnp.where(work == mx, lane, 128), axis=-1, keepdims=True)
            pick = lane == first
            sel = jnp.where(pick, jnp.where(mx > 0.5 * NEG, 1.0, 0.0), sel)
            work = jnp.where(pick, -jnp.inf, work)
        sel_ref[...] = sel
        t = lax.broadcasted_iota(jnp.int32, (rows, 128), 0) % n_new
        lo = lax.dot_general(wqs_ref[0], _pad_rows(kn, 128).astype(BF16), _NT, preferred_element_type=F32)
        lo = jnp.where(lane <= t, lo, NEG)
        m0 = jnp.max(lo, axis=-1, keepdims=True)
        p = jnp.exp(lo - m0)
        m_ref[...] = jnp.broadcast_to(m0, m_ref.shape)
        l_ref[...] = jnp.broadcast_to(jnp.sum(p, axis=-1, keepdims=True), l_ref.shape)
        acc_ref[...] = jnp.dot(p.astype(BF16), _pad_rows(vn_ref[0], 128).astype(BF16), preferred_element_type=F32)

    sel = sel_ref[...]
    lgs = []
    for j in range(STEP_BLOCKS):
        picked = jnp.max(jnp.where(lane == n * STEP_BLOCKS + j, sel, 0.0), axis=-1, keepdims=True) > 0.5
        lgs.append(jnp.where(picked, lg_ref[0, j], NEG))
    m_old = m_ref[:, :1]
    m_new = functools.reduce(jnp.maximum, [m_old] + [jnp.max(lg, axis=-1, keepdims=True) for lg in lgs])
    alpha = jnp.exp(m_old - m_new)
    ps = [jnp.exp(lg - m_new) for lg in lgs]
    m_ref[...] = jnp.broadcast_to(m_new, m_ref.shape)
    l_ref[...] = alpha * l_ref[...] + functools.reduce(
        lambda x, y: x + y, [jnp.sum(p, axis=-1, keepdims=True) for p in ps])
    values = jnp.concatenate([_page_rows(r) for r in v_refs], axis=0)
    acc_ref[...] = alpha * acc_ref[...] + jnp.dot(
        jnp.concatenate(ps, axis=1).astype(BF16), values, preferred_element_type=F32)

    @pl.when(n == pl.num_programs(1) - 1)
    def _():
        out = acc_ref[...] / l_ref[:, :1]
        for h in range(ATTN_HEADS):
            o_ref[0, :, h * ATTN_DH:(h + 1) * ATTN_DH] = (
                out[h * n_new:(h + 1) * n_new, h * ATTN_DH:(h + 1) * ATTN_DH].astype(o_ref.dtype))


def moba_sample(q, k_new, v_new, cache_k, cache_v, page_table):
    b, l, d = q.shape
    n_pages = PAST_LEN // PAGE_SIZE
    assert n_pages % STEP_PAGES == 0 and (PAST_LEN + l - 1) // MOBA_BLOCK == PAST_LEN // MOBA_BLOCK
    nblk = n_pages // PAGES_PER_BLOCK
    steps = n_pages // STEP_PAGES
    rows = ATTN_HEADS * l
    page_rows = PAGE_SIZE * ATTN_HEADS
    ck = cache_k.reshape(-1, ATTN_DH)
    cv = cache_v.reshape(-1, ATTN_DH)
    wq = _block_diag_queries(q.reshape(b, l, ATTN_HEADS, ATTN_DH))
    wqs = (wq * ATTN_DH ** -0.5).astype(BF16)

    per_batch = lambda r, c: pl.BlockSpec((1, r, c), lambda bi, n, pt: (bi, 0, 0))
    pages = [pl.BlockSpec((page_rows, ATTN_DH), functools.partial(
        lambda bi, n, pt, j: (pt[bi, STEP_PAGES * n + j], 0), j=j)) for j in range(STEP_PAGES)]
    logit_spec = pl.BlockSpec((1, STEP_BLOCKS, rows, MOBA_BLOCK), lambda bi, n, pt: (bi, n, 0, 0))

    bsum, logits = pl.pallas_call(
        _moba_sample_keys_kernel,
        grid_spec=pltpu.PrefetchScalarGridSpec(
            num_scalar_prefetch=1,
            grid=(b, steps),
            in_specs=[per_batch(rows, d)] + pages,
            out_specs=[pl.BlockSpec((1, STEP_BLOCKS, ATTN_HEADS, ATTN_DH), lambda bi, n, pt: (bi, n, 0, 0)),
                       logit_spec]),
        out_shape=[jax.ShapeDtypeStruct((b, nblk, ATTN_HEADS, ATTN_DH), F32),
                   jax.ShapeDtypeStruct((b, nblk, rows, MOBA_BLOCK), F32)],
        compiler_params=_params("parallel", "arbitrary"),
        name="moba_sample_keys",
    )(page_table, wqs, *([ck] * STEP_PAGES))

    return pl.pallas_call(
        functools.partial(_moba_sample_values_kernel, n_new=l, n_past_blocks=nblk),
        grid_spec=pltpu.PrefetchScalarGridSpec(
            num_scalar_prefetch=1,
            grid=(b, steps),
            in_specs=[per_batch(rows, d), per_batch(rows, d), per_batch(nblk, d), per_batch(l, d), per_batch(l, d),
                      logit_spec] + pages,
            out_specs=per_batch(l, d),
            scratch_shapes=[pltpu.VMEM((rows, 128), F32), pltpu.VMEM((rows, 128), F32),
                            pltpu.VMEM((rows, 128), F32), pltpu.VMEM((rows, d), F32)]),
        out_shape=jax.ShapeDtypeStruct((b, l, d), F32),
        compiler_params=_params("parallel", "arbitrary"),
        name="moba_sample_values",
    )(page_table, wq, wqs, bsum.reshape(b, nblk, d), k_new, v_new, logits, *([cv] * STEP_PAGES))


def kernel(x_prompt, x_sample, state_ret, cache_k, cache_v, page_table, ret_norm_g, ret_w_in, ret_w_out,
           attn_norm_g, attn_w_q, attn_w_out, kv_norm_g, w_kv, mlp_norm_g, mlp_w_up, mlp_w_down, final_norm_g):
    bp, lp, d = x_prompt.shape
    bs, ls, _ = x_sample.shape
    assert ret_w_in.shape[0] == 1 and attn_w_q.shape[0] == 1 and mlp_w_up.shape[0] == 2
    w_in = ret_w_in[0].astype(BF16)
    w_out = ret_w_out[0].astype(BF16)
    w_q = attn_w_q[0].astype(BF16)
    w_o = attn_w_out[0].astype(BF16)
    w_k = w_kv[:, :d].astype(BF16)
    w_v = w_kv[:, d:].astype(BF16)
    w_up = mlp_w_up.astype(BF16)
    w_down = mlp_w_down.astype(BF16)

    def trunk(x, pos, s0, attend):
        b, l, _ = x.shape
        h = x.reshape(b * l, d)
        act = BF16 if l % RET_ROWS == 0 else F32
        proj = norm_matmul(h, ret_norm_g[0], w_in, act)
        o, s = retention(proj.reshape(b, l, -1), pos, s0)
        h = matmul_residual(o.reshape(b * l, RET_V).astype(BF16), w_out, h)
        h = mlp(h, mlp_norm_g[0], w_up[0], w_down[0])
        k, v, q = kvq_proj(h, kv_norm_g, attn_norm_g[0], w_k, w_v, w_q)
        o = attend(q.reshape(b, l, d), k.reshape(b, l, d), v.reshape(b, l, d))
        h = matmul_residual(o.reshape(b * l, d).astype(BF16), w_o, h)
        y = mlp(h, mlp_norm_g[1], w_up[1], w_down[1], g_final=final_norm_g)
        kv_shape = (b, l, ATTN_HEADS, ATTN_DH)
        return y.reshape(b, l, d), s[None], k.reshape(kv_shape), v.reshape(kv_shape)

    y_p, s_p, k_p, v_p = trunk(x_prompt, jnp.arange(lp, dtype=jnp.int32), None, moba_prompt)
    y_s, s_s, k_s, v_s = trunk(
        x_sample, PAST_LEN + jnp.arange(ls, dtype=jnp.int32), state_ret[0],
        lambda q, k, v: moba_sample(q, k, v, cache_k, cache_v, page_table))
    return (y_p, y_s, s_p, s_s, k_p, v_p, k_s, v_s)
```

```python
import functools
import math

import jax
import jax.numpy as jnp
from jax import lax
from jax.experimental import pallas as pl
from jax.experimental.pallas import tpu as pltpu

F32 = jnp.float32
BF16 = jnp.bfloat16

D_MODEL = 1024
PAST_LEN = 8192
PAGE_SIZE = 128
RET_HEADS = 4
RET_DK = D_MODEL // RET_HEADS
RET_QK = RET_HEADS * RET_DK
RET_V = 2 * D_MODEL
RET_DV = RET_V // RET_HEADS
RET_CHUNK = 128
ATTN_HEADS = 8
ATTN_DH = D_MODEL // ATTN_HEADS
MOBA_BLOCK = 256
MOBA_TOPK = 3
D_FF = 4 * D_MODEL
NORM_EPS = 1e-5
GN_EPS = 1e-6
NEG = -1e30
LOG2_E = math.log2(math.e)

PAGES_PER_BLOCK = MOBA_BLOCK // PAGE_SIZE
STEP_PAGES = 8
STEP_BLOCKS = STEP_PAGES // PAGES_PER_BLOCK
RING_PAGES = 3 * STEP_PAGES
RET_ROWS = 128
VMEM_LIMIT_BYTES = 56 * 1024 * 1024

_NT = (((1,), (1,)), ((), ()))
_TN = (((0,), (0,)), ((), ()))


def _params(*sem):
    return pltpu.CompilerParams(dimension_semantics=sem, vmem_limit_bytes=VMEM_LIMIT_BYTES)


def _rms(x, g, eps=NORM_EPS):
    return x * lax.rsqrt(jnp.mean(x * x, axis=-1, keepdims=True) + eps) * g


def _norm_matmul_kernel(x_ref, g_ref, w_ref, o_ref, xn_ref):
    @pl.when(pl.program_id(1) == 0)
    def _():
        xn_ref[...] = _rms(x_ref[...], g_ref[...]).astype(BF16)

    o_ref[...] = jnp.dot(xn_ref[...], w_ref[...].astype(BF16), preferred_element_type=F32).astype(o_ref.dtype)


def norm_matmul(x, g, w, out_dtype, tm=1024, tn=2048):
    m, d = x.shape
    n = w.shape[1]
    tm = min(tm, m)
    return pl.pallas_call(
        _norm_matmul_kernel,
        grid=(m // tm, n // tn),
        in_specs=[pl.BlockSpec((tm, d), lambda i, j: (i, 0)),
                  pl.BlockSpec((1, d), lambda i, j: (0, 0)),
                  pl.BlockSpec((d, tn), lambda i, j: (0, j))],
        out_specs=pl.BlockSpec((tm, tn), lambda i, j: (i, j)),
        out_shape=jax.ShapeDtypeStruct((m, n), out_dtype),
        scratch_shapes=[pltpu.VMEM((tm, d), BF16)],
        compiler_params=_params("parallel", "arbitrary"),
        name="norm_matmul",
    )(x, g.reshape(1, d), w)


def _resident_weight(shape):
    return pl.BlockSpec(shape, lambda i: (0,) * len(shape), pipeline_mode=pl.Buffered(1))


def _matmul_residual_kernel(a_ref, w_ref, r_ref, o_ref, wb_ref):
    @pl.when(pl.program_id(0) == 0)
    def _():
        wb_ref[...] = w_ref[...].astype(BF16)

    o_ref[...] = r_ref[...] + jnp.dot(a_ref[...], wb_ref[...], preferred_element_type=F32)


def matmul_residual(a, w, res, tm=1024):
    m, k = a.shape
    n = w.shape[1]
    tm = min(tm, m)
    return pl.pallas_call(
        _matmul_residual_kernel,
        grid=(m // tm,),
        in_specs=[pl.BlockSpec((tm, k), lambda i: (i, 0)),
                  _resident_weight((k, n)),
                  pl.BlockSpec((tm, n), lambda i: (i, 0))],
        out_specs=pl.BlockSpec((tm, n), lambda i: (i, 0)),
        out_shape=jax.ShapeDtypeStruct((m, n), F32),
        scratch_shapes=[pltpu.VMEM((k, n), BF16)],
        compiler_params=_params("arbitrary"),
        name="matmul_residual",
    )(a, w, res)


def _mlp_kernel(x_ref, g_ref, wu_ref, wd_ref, gf_ref, o_ref, xn_ref, acc_ref, *, final_norm):
    f = pl.program_id(1)

    @pl.when(f == 0)
    def _():
        xn_ref[...] = _rms(x_ref[...], g_ref[...]).astype(BF16)
        acc_ref[...] = jnp.zeros_like(acc_ref)

    u = jnp.maximum(jnp.dot(xn_ref[...], wu_ref[...].astype(BF16), preferred_element_type=F32), 0.0)
    acc_ref[...] += jnp.dot((u * u).astype(BF16), wd_ref[...].astype(BF16), preferred_element_type=F32)

    @pl.when(f == pl.num_programs(1) - 1)
    def _():
        h = x_ref[...] + acc_ref[...]
        o_ref[...] = _rms(h, gf_ref[...]) if final_norm else h


def mlp(x, g, w_up, w_down, layer, g_final=None, tm=1024, tf=1024):
    m, d = x.shape
    ff = w_up.shape[2]
    tm = min(tm, m)
    final_norm = g_final is not None
    gf = g_final if final_norm else g
    return pl.pallas_call(
        functools.partial(_mlp_kernel, final_norm=final_norm),
        grid=(m // tm, ff // tf),
        in_specs=[pl.BlockSpec((tm, d), lambda i, f: (i, 0)),
                  pl.BlockSpec((1, d), lambda i, f: (0, 0)),
                  pl.BlockSpec((None, d, tf), lambda i, f: (layer, 0, f)),
                  pl.BlockSpec((None, tf, d), lambda i, f: (layer, f, 0)),
                  pl.BlockSpec((1, d), lambda i, f: (0, 0))],
        out_specs=pl.BlockSpec((tm, d), lambda i, f: (i, 0)),
        out_shape=jax.ShapeDtypeStruct((m, d), F32),
        scratch_shapes=[pltpu.VMEM((tm, d), BF16), pltpu.VMEM((tm, d), F32)],
        compiler_params=_params("parallel", "arbitrary"),
        name="mlp",
    )(x, g.reshape(1, d), w_up, w_down, gf.reshape(1, d))


def _kvq_kernel(x_ref, gkv_ref, gq_ref, wkv_ref, wq_ref, k_ref, v_ref, q_ref, wkvb_ref, wqb_ref):
    @pl.when(pl.program_id(0) == 0)
    def _():
        wkvb_ref[...] = wkv_ref[...].astype(BF16)
        wqb_ref[...] = wq_ref[...].astype(BF16)

    d = x_ref.shape[1]
    x = x_ref[...]
    r = x * lax.rsqrt(jnp.mean(x * x, axis=-1, keepdims=True) + NORM_EPS)
    xkv = (r * gkv_ref[...]).astype(BF16)
    xq = (r * gq_ref[...]).astype(BF16)
    k_ref[...] = jnp.dot(xkv, wkvb_ref[:, :d], preferred_element_type=F32)
    v_ref[...] = jnp.dot(xkv, wkvb_ref[:, d:], preferred_element_type=F32)
    q_ref[...] = jnp.dot(xq, wqb_ref[...], preferred_element_type=F32)


def kvq_proj(x, g_kv, g_q, w_kv, w_q, tm=512):
    m, d = x.shape
    tm = min(tm, m)
    row = pl.BlockSpec((tm, d), lambda i: (i, 0))
    gain = pl.BlockSpec((1, d), lambda i: (0, 0))
    out = jax.ShapeDtypeStruct((m, d), F32)
    return pl.pallas_call(
        _kvq_kernel,
        grid=(m // tm,),
        in_specs=[row, gain, gain, _resident_weight((d, 2 * d)), _resident_weight((d, d))],
        out_specs=[row, row, row],
        out_shape=[out, out, out],
        scratch_shapes=[pltpu.VMEM((d, 2 * d), BF16), pltpu.VMEM((d, d), BF16)],
        compiler_params=_params("arbitrary"),
        name="kvq_proj",
    )(x, g_kv.reshape(1, d), g_q.reshape(1, d), w_kv, w_q)


def _ret_log_decay():
    return jnp.log(1.0 - 2.0 ** (-5.0 - jnp.arange(RET_HEADS, dtype=F32)))


def _ret_tables(pos, chunk):
    angle = 1.0 / (10000.0 ** jnp.linspace(0.0, 1.0, RET_DK // 2, dtype=F32))
    angle = jnp.repeat(angle, 2)
    ang = pos.astype(F32)[:, None] * angle[None, :]
    sin, cos = jnp.sin(ang), jnp.cos(ang)
    even = (jnp.arange(RET_DK) % 2 == 0)[None, :]
    sin_next = jnp.where(even, -sin, 0.0)
    sin_prev = jnp.where(even, 0.0, sin)

    lg = _ret_log_decay()
    i = jnp.arange(chunk, dtype=F32)
    diff = i[:, None] - i[None, :]
    d_intra = jnp.exp(jnp.where(diff >= 0, lg[:, None, None] * diff, -jnp.inf))
    q_dec = jnp.exp(lg[:, None] * (i + 1.0))
    k_dec = jnp.exp(lg[:, None] * (chunk - 1.0 - i))
    c_dec = jnp.exp(lg * chunk)
    pad = RET_ROWS - chunk
    d_intra = jnp.pad(d_intra, ((0, 0), (0, pad), (0, pad)))
    q_dec = jnp.broadcast_to(jnp.pad(q_dec, ((0, 0), (0, pad)))[:, :, None], (RET_HEADS, RET_ROWS, RET_DK))
    k_dec = jnp.broadcast_to(jnp.pad(k_dec, ((0, 0), (0, pad)))[:, :, None], (RET_HEADS, RET_ROWS, RET_DK))
    c_dec = jnp.broadcast_to(c_dec[:, None, None], (RET_HEADS, 1, RET_DV))
    return cos, sin_next, sin_prev, d_intra, q_dec, k_dec, c_dec


def _theta_shift(x, cos, sin_next, sin_prev):
    dk = x.shape[-1]
    return x * cos + pltpu.roll(x, dk - 1, 1) * sin_next + pltpu.roll(x, 1, 1) * sin_prev


def _pad_rows(x, rows):
    if x.shape[0] == rows:
        return x
    return jnp.concatenate([x, jnp.zeros((rows - x.shape[0], x.shape[1]), x.dtype)], axis=0)


def _retention_kernel(q_ref, k_ref, v_ref, gate_ref, cos_ref, sn_ref, sp_ref, dintra_ref, qdec_ref, kdec_ref,
                      cdec_ref, s0_ref, o_ref, sout_ref, s_ref, *, chunk, has_state):
    c = pl.program_id(1)

    @pl.when(c == 0)
    def _():
        s_ref[...] = s0_ref[0] if has_state else jnp.zeros_like(s_ref)

    cos, sn, sp = cos_ref[...], sn_ref[...], sp_ref[...]
    for h in range(RET_HEADS):
        qk_cols = slice(h * RET_DK, (h + 1) * RET_DK)
        v_cols = slice(h * RET_DV, (h + 1) * RET_DV)
        q = _theta_shift(q_ref[0, :, qk_cols].astype(F32), cos, sn, sp)
        k = _theta_shift(k_ref[0, :, qk_cols].astype(F32), cos, sn, sp) * (RET_DK ** -0.5)
        q = _pad_rows(q, RET_ROWS)
        k = _pad_rows(k, RET_ROWS)
        v = _pad_rows(v_ref[0, :, v_cols], RET_ROWS).astype(BF16)
        s = s_ref[h]

        a = lax.dot_general(q.astype(BF16), k.astype(BF16), _NT, preferred_element_type=F32) * dintra_ref[h]
        o = (jnp.dot(a.astype(BF16), v, preferred_element_type=F32)
             + jnp.dot((q * qdec_ref[h]).astype(BF16), s.astype(BF16), preferred_element_type=F32))
        s_ref[h] = s * cdec_ref[h] + lax.dot_general((k * kdec_ref[h]).astype(BF16), v, _TN,
                                                     preferred_element_type=F32)

        o = o[:chunk]
        o = o * lax.rsqrt(jnp.mean(o * o, axis=-1, keepdims=True) + GN_EPS)
        gate = gate_ref[0, :, v_cols].astype(F32)
        o_ref[0, :, v_cols] = (o * (gate * jax.nn.sigmoid(gate))).astype(o_ref.dtype)

    @pl.when(c == pl.num_programs(1) - 1)
    def _():
        sout_ref[0] = s_ref[...]


def retention(proj, pos, s0):
    b, l, _ = proj.shape
    chunk = math.gcd(l, RET_CHUNK)
    n = l // chunk
    has_state = s0 is not None
    if not has_state:
        s0 = jnp.zeros((1, RET_HEADS, RET_DK, RET_DV), F32)
    tables = _ret_tables(pos, chunk)
    k_col, v_col, gate_col = 1, 2 * RET_QK // RET_V, 2 * RET_QK // RET_V + 1
    pos_tab = pl.BlockSpec((chunk, RET_DK), lambda bi, c: (c, 0))
    head_tab = lambda rows, cols: pl.BlockSpec((RET_HEADS, rows, cols), lambda bi, c: (0, 0, 0))
    state = lambda index_map: pl.BlockSpec((1, RET_HEADS, RET_DK, RET_DV), index_map)
    return pl.pallas_call(
        functools.partial(_retention_kernel, chunk=chunk, has_state=has_state),
        grid=(b, n),
        in_specs=[pl.BlockSpec((1, chunk, RET_QK), lambda bi, c: (bi, c, 0)),
                  pl.BlockSpec((1, chunk, RET_QK), lambda bi, c: (bi, c, k_col)),
                  pl.BlockSpec((1, chunk, RET_V), lambda bi, c: (bi, c, v_col)),
                  pl.BlockSpec((1, chunk, RET_V), lambda bi, c: (bi, c, gate_col)),
                  pos_tab, pos_tab, pos_tab,
                  head_tab(RET_ROWS, RET_ROWS), head_tab(RET_ROWS, RET_DK), head_tab(RET_ROWS, RET_DK),
                  head_tab(1, RET_DV),
                  state((lambda bi, c: (bi, 0, 0, 0)) if has_state else (lambda bi, c: (0, 0, 0, 0)))],
        out_specs=[pl.BlockSpec((1, chunk, RET_V), lambda bi, c: (bi, c, 0)),
                   state(lambda bi, c: (bi, 0, 0, 0))],
        out_shape=[jax.ShapeDtypeStruct((b, l, RET_V), proj.dtype),
                   jax.ShapeDtypeStruct((b, RET_HEADS, RET_DK, RET_DV), F32)],
        scratch_shapes=[pltpu.VMEM((RET_HEADS, RET_DK, RET_DV), F32)],
        compiler_params=_params("parallel", "arbitrary"),
        name="retention",
    )(proj, proj, proj, proj, *tables, s0)


def _moba_prompt_kernel(q_ref, k_ref, v_ref, o_ref, *, nb):
    k = k_ref[0]
    s_len = k.shape[0]
    means = jnp.sum(k.reshape(nb, MOBA_BLOCK, ATTN_DH), axis=1) * (1.0 / MOBA_BLOCK)
    means = _pad_rows(means, -(-nb // 8) * 8)
    blk = lax.broadcasted_iota(jnp.int32, (means.shape[0], MOBA_BLOCK), 0)
    key_lane = lax.broadcasted_iota(jnp.int32, (s_len, 128), 1)
    key_block = lax.broadcasted_iota(jnp.int32, (s_len, 128), 0) // MOBA_BLOCK
    kb = jnp.concatenate([k.astype(BF16), jnp.where(key_block == key_lane, 1.0, 0.0).astype(BF16)], axis=1)
    vb = jnp.concatenate([v_ref[0].astype(BF16), jnp.where(key_lane == 0, 1.0, 0.0).astype(BF16)], axis=1)
    row = lax.broadcasted_iota(jnp.int32, (MOBA_BLOCK, MOBA_BLOCK), 0)
    col = lax.broadcasted_iota(jnp.int32, (MOBA_BLOCK, MOBA_BLOCK), 1)
    causal = col <= row
    eye = jnp.where(col == row, 1.0, 0.0).astype(BF16)

    for i in range(nb):
        q = q_ref[0, i * MOBA_BLOCK:(i + 1) * MOBA_BLOCK, :]
        bias = jnp.zeros((MOBA_BLOCK, 128), BF16)
        if i > 0:
            s = lax.dot_general(means, q, _NT, preferred_element_type=F32, precision=lax.Precision.HIGHEST)
            elig = blk < i
            sm = jnp.where(elig, s, NEG)
            rank = jnp.zeros(sm.shape, F32)
            for m in range(i):
                cm = sm[m:m + 1, :]
                beats = jnp.where(cm > sm, 1.0, jnp.where(cm == sm, jnp.where(blk > m, 1.0, 0.0), 0.0))
                rank = rank + beats
            keep = jnp.where(rank < MOBA_TOPK, jnp.where(sm > 0.5 * NEG, 0.0, NEG), NEG)
            bias_t = _pad_rows(jnp.where(elig, keep, 0.0), 128).astype(BF16)
            bias = lax.dot_general(eye, bias_t, _NT, preferred_element_type=F32).astype(BF16)
        qa = jnp.concatenate([(q * (ATTN_DH ** -0.5 * LOG2_E)).astype(BF16), bias], axis=1)
        kv_rows = (i + 1) * MOBA_BLOCK
        logits = lax.dot_general(qa, kb[:kv_rows], _NT, preferred_element_type=F32)
        own = jnp.where(causal, logits[:, i * MOBA_BLOCK:], NEG)
        logits = own if i == 0 else jnp.concatenate([logits[:, :i * MOBA_BLOCK], own], axis=1)
        p = jnp.exp2(logits - jnp.max(logits, axis=-1, keepdims=True))
        acc = jnp.dot(p.astype(BF16), vb[:kv_rows], preferred_element_type=F32)
        o_ref[0, i * MOBA_BLOCK:(i + 1) * MOBA_BLOCK, :] = (
            acc[:, :ATTN_DH] / acc[:, ATTN_DH:ATTN_DH + 1]).astype(o_ref.dtype)


def moba_prompt(q, k, v):
    b, s, _ = q.shape
    assert s % MOBA_BLOCK == 0
    nb = s // MOBA_BLOCK
    spec = pl.BlockSpec((1, s, ATTN_DH), lambda bi, h: (bi, 0, h))
    return pl.pallas_call(
        functools.partial(_moba_prompt_kernel, nb=nb),
        grid=(b, ATTN_HEADS),
        in_specs=[spec, spec, spec],
        out_specs=spec,
        out_shape=jax.ShapeDtypeStruct((b, s, ATTN_HEADS * ATTN_DH), BF16),
        compiler_params=_params("parallel", "parallel"),
        name="moba_prompt",
    )(q, k, v)


def _block_diag_queries(q):
    b, l, h, dh = q.shape
    eye = jnp.eye(h, dtype=q.dtype)
    w = q.transpose(0, 2, 1, 3)[:, :, :, None, :] * eye[None, :, None, :, None]
    return w.reshape(b, h * l, h * dh)


def _page_rows(page_ref):
    heads = [page_ref[pl.ds(h, PAGE_SIZE, stride=ATTN_HEADS), :] for h in range(ATTN_HEADS)]
    return jnp.concatenate(heads, axis=1).astype(BF16)


class _PageStream:
    def __init__(self, pt_ref, cache_ref, buf_ref, sem_ref, n_pages):
        self.pt_ref, self.cache_ref, self.buf_ref, self.sem_ref = pt_ref, cache_ref, buf_ref, sem_ref
        self.n_pages = n_pages
        self.total = pl.num_programs(0) * n_pages
        self.base = pl.program_id(0) * n_pages

    def _copy(self, page, g):
        slot = g % RING_PAGES
        return pltpu.make_async_copy(self.cache_ref.at[page], self.buf_ref.at[slot], self.sem_ref.at[slot])

    def _start(self, g):
        self._copy(self.pt_ref[g // self.n_pages, g % self.n_pages], g).start()

    def prime(self):
        @pl.when(pl.program_id(0) == 0)
        def _():
            for g in range(RING_PAGES - STEP_PAGES):
                self._start(g)

    def group(self, it):
        g0 = self.base + it * STEP_PAGES
        for j in range(STEP_PAGES):
            ahead = g0 + (RING_PAGES - STEP_PAGES) + j

            @pl.when(ahead < self.total)
            def _(ahead=ahead):
                self._start(ahead)

        pages = []
        for j in range(STEP_PAGES):
            self._copy(0, g0 + j).wait()
            pages.append(self.buf_ref.at[(g0 + j) % RING_PAGES])
        return pages


def _moba_sample_keys_kernel(pt_ref, wq_ref, ck_ref, bsum_ref, lg_ref, buf_ref, sem_ref, *, n_pages):
    stream = _PageStream(pt_ref, ck_ref, buf_ref, sem_ref, n_pages)
    stream.prime()
    w = wq_ref[0]

    def body(it, carry):
        pages = stream.group(it)
        sums = [jnp.sum(p[...].reshape(PAGE_SIZE, ATTN_HEADS, ATTN_DH), axis=0) for p in pages]
        keys = jnp.concatenate([_page_rows(p) for p in pages], axis=0)
        logits = lax.dot_general(w, keys, _NT, preferred_element_type=F32)
        for j in range(STEP_BLOCKS):
            blk = it * STEP_BLOCKS + j
            bsum_ref[0, blk] = functools.reduce(
                lambda x, y: x + y, sums[j * PAGES_PER_BLOCK:(j + 1) * PAGES_PER_BLOCK])
            lg_ref[0, blk] = logits[:, j * MOBA_BLOCK:(j + 1) * MOBA_BLOCK]
        return carry

    lax.fori_loop(0, n_pages // STEP_PAGES, body, 0)


def _moba_sample_values_kernel(pt_ref, wq_ref, wqs_ref, bsum_ref, kn_ref, vn_ref, lg_ref, cv_ref, o_ref,
                               buf_ref, sem_ref, sel_ref, m_ref, l_ref, acc_ref, *, n_new, n_past_blocks, n_pages):
    stream = _PageStream(pt_ref, cv_ref, buf_ref, sem_ref, n_pages)
    stream.prime()
    rows = wq_ref.shape[1]
    lane = lax.broadcasted_iota(jnp.int32, (rows, 128), 1)

    kn = kn_ref[0]
    first_row = lax.broadcasted_iota(jnp.int32, (8, kn.shape[1]), 0) == 0
    own_sum = jnp.where(first_row, jnp.sum(kn, axis=0, keepdims=True), 0.0)
    means = _pad_rows(jnp.concatenate([bsum_ref[0], own_sum], axis=0) * (1.0 / MOBA_BLOCK), 128)
    s = lax.dot_general(wq_ref[0], means, _NT, preferred_element_type=F32, precision=lax.Precision.HIGHEST)
    work = jnp.where(lane < n_past_blocks, s, NEG)
    sel = jnp.zeros(work.shape, F32)
    for _ in range(MOBA_TOPK):
        mx = jnp.max(work, axis=-1, keepdims=True)
        first = jnp.min(jnp.where(work == mx, lane, 128), axis=-1, keepdims=True)
        pick = lane == first
        sel = jnp.where(pick, jnp.where(mx > 0.5 * NEG, 1.0, 0.0), sel)
        work = jnp.where(pick, -jnp.inf, work)
    sel_ref[...] = sel
    t = lax.broadcasted_iota(jnp.int32, (rows, 128), 0) % n_new
    lo = lax.dot_general(wqs_ref[0], _pad_rows(kn, 128).astype(BF16), _NT, preferred_element_type=F32)
    lo = jnp.where(lane <= t, lo, NEG)
    m0 = jnp.max(lo, axis=-1, keepdims=True)
    p0 = jnp.exp(lo - m0)
    m_ref[...] = jnp.broadcast_to(m0, m_ref.shape)
    l_ref[...] = jnp.broadcast_to(jnp.sum(p0, axis=-1, keepdims=True), l_ref.shape)
    acc_ref[...] = jnp.dot(p0.astype(BF16), _pad_rows(vn_ref[0], 128).astype(BF16), preferred_element_type=F32)

    def body(it, carry):
        pages = stream.group(it)
        sel = sel_ref[...]
        lgs = []
        for j in range(STEP_BLOCKS):
            blk = it * STEP_BLOCKS + j
            picked = jnp.max(jnp.where(lane == blk, sel, 0.0), axis=-1, keepdims=True) > 0.5
            lgs.append(jnp.where(picked, lg_ref[0, blk], NEG))
        m_old = m_ref[:, :1]
        m_new = functools.reduce(jnp.maximum, [m_old] + [jnp.max(lg, axis=-1, keepdims=True) for lg in lgs])
        alpha = jnp.exp(m_old - m_new)
        ps = [jnp.exp(lg - m_new) for lg in lgs]
        m_ref[...] = jnp.broadcast_to(m_new, m_ref.shape)
        l_ref[...] = alpha * l_ref[...] + functools.reduce(
            lambda x, y: x + y, [jnp.sum(p, axis=-1, keepdims=True) for p in ps])
        values = jnp.concatenate([_page_rows(p) for p in pages], axis=0)
        acc_ref[...] = alpha * acc_ref[...] + jnp.dot(
            jnp.concatenate(ps, axis=1).astype(BF16), values, preferred_element_type=F32)
        return carry

    lax.fori_loop(0, n_pages // STEP_PAGES, body, 0)

    out = acc_ref[...] / l_ref[:, :1]
    for h in range(ATTN_HEADS):
        o_ref[0, :, h * ATTN_DH:(h + 1) * ATTN_DH] = (
            out[h * n_new:(h + 1) * n_new, h * ATTN_DH:(h + 1) * ATTN_DH].astype(o_ref.dtype))


def moba_sample(q, k_new, v_new, cache_k, cache_v, page_table):
    b, l, d = q.shape
    n_pages = PAST_LEN // PAGE_SIZE
    assert n_pages % STEP_PAGES == 0 and (PAST_LEN + l - 1) // MOBA_BLOCK == PAST_LEN // MOBA_BLOCK
    nblk = n_pages // PAGES_PER_BLOCK
    rows = ATTN_HEADS * l
    page_rows = PAGE_SIZE * ATTN_HEADS
    ck = cache_k.reshape(-1, page_rows, ATTN_DH)
    cv = cache_v.reshape(-1, page_rows, ATTN_DH)
    wq = _block_diag_queries(q.reshape(b, l, ATTN_HEADS, ATTN_DH))
    wqs = (wq * ATTN_DH ** -0.5).astype(BF16)

    per_batch = lambda *shape: pl.BlockSpec((1,) + shape, lambda bi, pt: (bi,) + (0,) * len(shape))
    cache = pl.BlockSpec(memory_space=pl.ANY)
    ring = [pltpu.VMEM((RING_PAGES, page_rows, ATTN_DH), F32), pltpu.SemaphoreType.DMA((RING_PAGES,))]

    bsum, logits = pl.pallas_call(
        functools.partial(_moba_sample_keys_kernel, n_pages=n_pages),
        grid_spec=pltpu.PrefetchScalarGridSpec(
            num_scalar_prefetch=1,
            grid=(b,),
            in_specs=[per_batch(rows, d), cache],
            out_specs=[per_batch(nblk, ATTN_HEADS, ATTN_DH), per_batch(nblk, rows, MOBA_BLOCK)],
            scratch_shapes=ring),
        out_shape=[jax.ShapeDtypeStruct((b, nblk, ATTN_HEADS, ATTN_DH), F32),
                   jax.ShapeDtypeStruct((b, nblk, rows, MOBA_BLOCK), F32)],
        compiler_params=_params("arbitrary"),
        name="moba_sample_keys",
    )(page_table, wqs, ck)

    return pl.pallas_call(
        functools.partial(_moba_sample_values_kernel, n_new=l, n_past_blocks=nblk, n_pages=n_pages),
        grid_spec=pltpu.PrefetchScalarGridSpec(
            num_scalar_prefetch=1,
            grid=(b,),
            in_specs=[per_batch(rows, d), per_batch(rows, d), per_batch(nblk, d), per_batch(l, d), per_batch(l, d),
                      per_batch(nblk, rows, MOBA_BLOCK), cache],
            out_specs=per_batch(l, d),
            scratch_shapes=ring + [pltpu.VMEM((rows, 128), F32), pltpu.VMEM((rows, 128), F32),
                                   pltpu.VMEM((rows, 128), F32), pltpu.VMEM((rows, d), F32)]),
        out_shape=jax.ShapeDtypeStruct((b, l, d), F32),
        compiler_params=_params("arbitrary"),
        name="moba_sample_values",
    )(page_table, wq, wqs, bsum.reshape(b, nblk, d), k_new, v_new, logits, cv)


def kernel(x_prompt, x_sample, state_ret, cache_k, cache_v, page_table, ret_norm_g, ret_w_in, ret_w_out,
           attn_norm_g, attn_w_q, attn_w_out, kv_norm_g, w_kv, mlp_norm_g, mlp_w_up, mlp_w_down, final_norm_g):
    bp, lp, d = x_prompt.shape
    bs, ls, _ = x_sample.shape
    assert ret_w_in.shape[0] == 1 and attn_w_q.shape[0] == 1 and mlp_w_up.shape[0] == 2

    def trunk(x, pos, s0, attend):
        b, l, _ = x.shape
        h = x.reshape(b * l, d)
        act = BF16 if l % RET_ROWS == 0 else F32
        proj = norm_matmul(h, ret_norm_g[0], ret_w_in[0], act)
        o, s = retention(proj.reshape(b, l, -1), pos, s0)
        h = matmul_residual(o.reshape(b * l, RET_V).astype(BF16), ret_w_out[0], h)
        h = mlp(h, mlp_norm_g[0], mlp_w_up, mlp_w_down, 0)
        k, v, q = kvq_proj(h, kv_norm_g, attn_norm_g[0], w_kv, attn_w_q[0])
        o = attend(q.reshape(b, l, d), k.reshape(b, l, d), v.reshape(b, l, d))
        h = matmul_residual(o.reshape(b * l, d).astype(BF16), attn_w_out[0], h)
        y = mlp(h, mlp_norm_g[1], mlp_w_up, mlp_w_down, 1, g_final=final_norm_g)
        kv_shape = (b, l, ATTN_HEADS, ATTN_DH)
        return y.reshape(b, l, d), s[None], k.reshape(kv_shape), v.reshape(kv_shape)

    y_p, s_p, k_p, v_p = trunk(x_prompt, jnp.arange(lp, dtype=jnp.int32), None, moba_prompt)
    y_s, s_s, k_s, v_s = trunk(
        x_sample, PAST_LEN + jnp.arange(ls, dtype=jnp.int32), state_ret[0],
        lambda q, k, v: moba_sample(q, k, v, cache_k, cache_v, page_table))
    return (y_p, y_s, s_p, s_s, k_p, v_p, k_s, v_s)
```

```python
import functools
import math

import jax
import jax.numpy as jnp
from jax import lax
from jax.experimental import pallas as pl
from jax.experimental.pallas import tpu as pltpu

F32 = jnp.float32
BF16 = jnp.bfloat16

D_MODEL = 1024
PAST_LEN = 8192
PAGE_SIZE = 128
RET_HEADS = 4
RET_DK = D_MODEL // RET_HEADS
RET_QK = RET_HEADS * RET_DK
RET_V = 2 * D_MODEL
RET_DV = RET_V // RET_HEADS
RET_CHUNK = 128
ATTN_HEADS = 8
ATTN_DH = D_MODEL // ATTN_HEADS
MOBA_BLOCK = 256
MOBA_TOPK = 3
D_FF = 4 * D_MODEL
NORM_EPS = 1e-5
GN_EPS = 1e-6
NEG = -1e30
LOG2_E = math.log2(math.e)

PAGES_PER_BLOCK = MOBA_BLOCK // PAGE_SIZE
STEP_PAGES = 8
STEP_BLOCKS = STEP_PAGES // PAGES_PER_BLOCK
RING_PAGES = 3 * STEP_PAGES
ROW_CHUNKS = 4
RET_ROWS = 128
VMEM_LIMIT_BYTES = 56 * 1024 * 1024

_NT = (((1,), (1,)), ((), ()))
_TN = (((0,), (0,)), ((), ()))


def _params(*sem):
    return pltpu.CompilerParams(dimension_semantics=sem, vmem_limit_bytes=VMEM_LIMIT_BYTES)


def _rms(x, g, eps=NORM_EPS):
    return x * lax.rsqrt(jnp.mean(x * x, axis=-1, keepdims=True) + eps) * g


def _row_chunks(rows):
    size = rows // ROW_CHUNKS if rows % (ROW_CHUNKS * 16) == 0 else rows
    return [slice(r, r + size) for r in range(0, rows, size)]


def _norm_matmul_kernel(x_ref, g_ref, w_ref, o_ref, xn_ref):
    @pl.when(pl.program_id(1) == 0)
    def _():
        w = w_ref[...].astype(BF16)
        for rows in _row_chunks(x_ref.shape[0]):
            xn = _rms(x_ref[rows, :], g_ref[...]).astype(BF16)
            xn_ref[rows, :] = xn
            o_ref[rows, :] = jnp.dot(xn, w, preferred_element_type=F32).astype(o_ref.dtype)

    @pl.when(pl.program_id(1) != 0)
    def _():
        o_ref[...] = jnp.dot(xn_ref[...], w_ref[...].astype(BF16), preferred_element_type=F32).astype(o_ref.dtype)


def norm_matmul(x, g, w, out_dtype, tm=1024, tn=2048):
    m, d = x.shape
    n = w.shape[1]
    tm = min(tm, m)
    return pl.pallas_call(
        _norm_matmul_kernel,
        grid=(m // tm, n // tn),
        in_specs=[pl.BlockSpec((tm, d), lambda i, j: (i, 0)),
                  pl.BlockSpec((1, d), lambda i, j: (0, 0)),
                  pl.BlockSpec((d, tn), lambda i, j: (0, j))],
        out_specs=pl.BlockSpec((tm, tn), lambda i, j: (i, j)),
        out_shape=jax.ShapeDtypeStruct((m, n), out_dtype),
        scratch_shapes=[pltpu.VMEM((tm, d), BF16)],
        compiler_params=_params("parallel", "arbitrary"),
        name="norm_matmul",
    )(x, g.reshape(1, d), w)


def _resident_weight(shape):
    return pl.BlockSpec(shape, lambda i: (0,) * len(shape), pipeline_mode=pl.Buffered(1))


def _matmul_residual_kernel(a_ref, w_ref, r_ref, o_ref, wb_ref):
    @pl.when(pl.program_id(0) == 0)
    def _():
        wb_ref[...] = w_ref[...].astype(BF16)

    o_ref[...] = r_ref[...] + jnp.dot(a_ref[...], wb_ref[...], preferred_element_type=F32)


def matmul_residual(a, w, res, tm=1024):
    m, k = a.shape
    n = w.shape[1]
    tm = min(tm, m)
    return pl.pallas_call(
        _matmul_residual_kernel,
        grid=(m // tm,),
        in_specs=[pl.BlockSpec((tm, k), lambda i: (i, 0)),
                  _resident_weight((k, n)),
                  pl.BlockSpec((tm, n), lambda i: (i, 0))],
        out_specs=pl.BlockSpec((tm, n), lambda i: (i, 0)),
        out_shape=jax.ShapeDtypeStruct((m, n), F32),
        scratch_shapes=[pltpu.VMEM((k, n), BF16)],
        compiler_params=_params("arbitrary"),
        name="matmul_residual",
    )(a, w, res)


def _mlp_kernel(x_ref, g_ref, wu_ref, wd_ref, gf_ref, o_ref, xn_ref, acc_ref, *, final_norm):
    f = pl.program_id(1)
    last = pl.num_programs(1) - 1

    def weights():
        return wu_ref[...].astype(BF16), wd_ref[...].astype(BF16)

    def hidden(xn, wu, wd):
        u = jnp.maximum(jnp.dot(xn, wu, preferred_element_type=F32), 0.0)
        return jnp.dot((u * u).astype(BF16), wd, preferred_element_type=F32)

    @pl.when(f == 0)
    def _():
        wu, wd = weights()
        for rows in _row_chunks(x_ref.shape[0]):
            xn = _rms(x_ref[rows, :], g_ref[...]).astype(BF16)
            xn_ref[rows, :] = xn
            acc_ref[rows, :] = hidden(xn, wu, wd)

    @pl.when(jnp.logical_and(f != 0, f != last))
    def _():
        acc_ref[...] += hidden(xn_ref[...], *weights())

    @pl.when(f == last)
    def _():
        wu, wd = weights()
        for rows in _row_chunks(x_ref.shape[0]):
            h = x_ref[rows, :] + acc_ref[rows, :] + hidden(xn_ref[rows, :], wu, wd)
            o_ref[rows, :] = _rms(h, gf_ref[...]) if final_norm else h


def mlp(x, g, w_up, w_down, layer, g_final=None, tm=1024, tf=1024):
    m, d = x.shape
    ff = w_up.shape[2]
    assert ff // tf >= 2
    tm = min(tm, m)
    final_norm = g_final is not None
    gf = g_final if final_norm else g
    return pl.pallas_call(
        functools.partial(_mlp_kernel, final_norm=final_norm),
        grid=(m // tm, ff // tf),
        in_specs=[pl.BlockSpec((tm, d), lambda i, f: (i, 0)),
                  pl.BlockSpec((1, d), lambda i, f: (0, 0)),
                  pl.BlockSpec((None, d, tf), lambda i, f: (layer, 0, f)),
                  pl.BlockSpec((None, tf, d), lambda i, f: (layer, f, 0)),
                  pl.BlockSpec((1, d), lambda i, f: (0, 0))],
        out_specs=pl.BlockSpec((tm, d), lambda i, f: (i, 0)),
        out_shape=jax.ShapeDtypeStruct((m, d), F32),
        scratch_shapes=[pltpu.VMEM((tm, d), BF16), pltpu.VMEM((tm, d), F32)],
        compiler_params=_params("parallel", "arbitrary"),
        name="mlp",
    )(x, g.reshape(1, d), w_up, w_down, gf.reshape(1, d))


def _kvq_kernel(x_ref, gkv_ref, gq_ref, wkv_ref, wq_ref, k_ref, v_ref, q_ref, wkvb_ref, wqb_ref):
    @pl.when(pl.program_id(0) == 0)
    def _():
        wkvb_ref[...] = wkv_ref[...].astype(BF16)
        wqb_ref[...] = wq_ref[...].astype(BF16)

    d = x_ref.shape[1]
    x = x_ref[...]
    r = x * lax.rsqrt(jnp.mean(x * x, axis=-1, keepdims=True) + NORM_EPS)
    xkv = (r * gkv_ref[...]).astype(BF16)
    xq = (r * gq_ref[...]).astype(BF16)
    k_ref[...] = jnp.dot(xkv, wkvb_ref[:, :d], preferred_element_type=F32)
    v_ref[...] = jnp.dot(xkv, wkvb_ref[:, d:], preferred_element_type=F32)
    q_ref[...] = jnp.dot(xq, wqb_ref[...], preferred_element_type=F32)


def kvq_proj(x, g_kv, g_q, w_kv, w_q, tm=512):
    m, d = x.shape
    tm = min(tm, m)
    row = pl.BlockSpec((tm, d), lambda i: (i, 0))
    gain = pl.BlockSpec((1, d), lambda i: (0, 0))
    out = jax.ShapeDtypeStruct((m, d), F32)
    return pl.pallas_call(
        _kvq_kernel,
        grid=(m // tm,),
        in_specs=[row, gain, gain, _resident_weight((d, 2 * d)), _resident_weight((d, d))],
        out_specs=[row, row, row],
        out_shape=[out, out, out],
        scratch_shapes=[pltpu.VMEM((d, 2 * d), BF16), pltpu.VMEM((d, d), BF16)],
        compiler_params=_params("arbitrary"),
        name="kvq_proj",
    )(x, g_kv.reshape(1, d), g_q.reshape(1, d), w_kv, w_q)


def _ret_log_decay():
    return jnp.log(1.0 - 2.0 ** (-5.0 - jnp.arange(RET_HEADS, dtype=F32)))


def _ret_tables(pos, chunk):
    angle = 1.0 / (10000.0 ** jnp.linspace(0.0, 1.0, RET_DK // 2, dtype=F32))
    angle = jnp.repeat(angle, 2)
    ang = pos.astype(F32)[:, None] * angle[None, :]
    sin, cos = jnp.sin(ang), jnp.cos(ang)
    even = (jnp.arange(RET_DK) % 2 == 0)[None, :]
    sin_next = jnp.where(even, -sin, 0.0)
    sin_prev = jnp.where(even, 0.0, sin)

    lg = _ret_log_decay()
    i = jnp.arange(chunk, dtype=F32)
    diff = i[:, None] - i[None, :]
    d_intra = jnp.exp(jnp.where(diff >= 0, lg[:, None, None] * diff, -jnp.inf))
    q_dec = jnp.exp(lg[:, None] * (i + 1.0))
    k_dec = jnp.exp(lg[:, None] * (chunk - 1.0 - i))
    c_dec = jnp.exp(lg * chunk)
    d_intra = d_intra * RET_DK ** -0.5
    k_dec = k_dec * RET_DK ** -0.5
    pad = RET_ROWS - chunk
    d_intra = jnp.pad(d_intra, ((0, 0), (0, pad), (0, pad)))
    q_dec = jnp.broadcast_to(jnp.pad(q_dec, ((0, 0), (0, pad)))[:, :, None], (RET_HEADS, RET_ROWS, RET_DK))
    k_dec = jnp.broadcast_to(jnp.pad(k_dec, ((0, 0), (0, pad)))[:, :, None], (RET_HEADS, RET_ROWS, RET_DK))
    c_dec = jnp.broadcast_to(c_dec[:, None, None], (RET_HEADS, 1, RET_DV))
    return cos, sin_next, sin_prev, d_intra, q_dec, k_dec, c_dec


def _theta_shift(x, cos, sin_next, sin_prev):
    dk = x.shape[-1]
    return x * cos + pltpu.roll(x, dk - 1, 1) * sin_next + pltpu.roll(x, 1, 1) * sin_prev


def _pad_rows(x, rows):
    if x.shape[0] == rows:
        return x
    return jnp.concatenate([x, jnp.zeros((rows - x.shape[0], x.shape[1]), x.dtype)], axis=0)


def _retention_kernel(q_ref, k_ref, v_ref, gate_ref, cos_ref, sn_ref, sp_ref, dintra_ref, qdec_ref, kdec_ref,
                      cdec_ref, s0_ref, o_ref, sout_ref, *, chunk, n_chunks, has_state):
    seeded = n_chunks > 1
    if seeded:
        @pl.when(pl.program_id(1) == 0)
        def _():
            sout_ref[0] = s0_ref[0] if has_state else jnp.zeros(sout_ref.shape[1:], F32)

    cos, sn, sp = cos_ref[...], sn_ref[...], sp_ref[...]

    def scores(h):
        qk_cols = slice(h * RET_DK, (h + 1) * RET_DK)
        q = _theta_shift(q_ref[0, :, qk_cols].astype(F32), cos, sn, sp)
        k = _theta_shift(k_ref[0, :, qk_cols].astype(F32), cos, sn, sp)
        q = _pad_rows(q, RET_ROWS)
        k = _pad_rows(k, RET_ROWS)
        a = lax.dot_general(q.astype(BF16), k.astype(BF16), _NT, preferred_element_type=F32) * dintra_ref[h]
        return q, k, a

    def finish(h, q, k, a):
        v_cols = slice(h * RET_DV, (h + 1) * RET_DV)
        v = _pad_rows(v_ref[0, :, v_cols], RET_ROWS).astype(BF16)
        if seeded:
            s = sout_ref[0, h]
        else:
            s = s0_ref[0, h] if has_state else jnp.zeros((RET_DK, RET_DV), F32)
        o = (jnp.dot(a.astype(BF16), v, preferred_element_type=F32)
             + jnp.dot((q * qdec_ref[h]).astype(BF16), s.astype(BF16), preferred_element_type=F32))
        sout_ref[0, h] = s * cdec_ref[h] + lax.dot_general((k * kdec_ref[h]).astype(BF16), v, _TN,
                                                           preferred_element_type=F32)
        o = o[:chunk]
        o = o * lax.rsqrt(jnp.mean(o * o, axis=-1, keepdims=True) + GN_EPS)
        gate = gate_ref[0, :, v_cols].astype(F32)
        o_ref[0, :, v_cols] = (o * (gate * jax.nn.sigmoid(gate))).astype(o_ref.dtype)

    pending = scores(0)
    for h in range(RET_HEADS):
        following = scores(h + 1) if h + 1 < RET_HEADS else None
        finish(h, *pending)
        pending = following


def retention(proj, pos, s0):
    b, l, _ = proj.shape
    chunk = math.gcd(l, RET_CHUNK)
    n = l // chunk
    has_state = s0 is not None
    if not has_state:
        s0 = jnp.zeros((1, RET_HEADS, RET_DK, RET_DV), F32)
    tables = _ret_tables(pos, chunk)
    k_col, v_col, gate_col = 1, 2 * RET_QK // RET_V, 2 * RET_QK // RET_V + 1
    pos_tab = pl.BlockSpec((chunk, RET_DK), lambda bi, c: (c, 0))
    head_tab = lambda rows, cols: pl.BlockSpec((RET_HEADS, rows, cols), lambda bi, c: (0, 0, 0))
    state = lambda index_map: pl.BlockSpec((1, RET_HEADS, RET_DK, RET_DV), index_map)
    return pl.pallas_call(
        functools.partial(_retention_kernel, chunk=chunk, n_chunks=n, has_state=has_state),
        grid=(b, n),
        in_specs=[pl.BlockSpec((1, chunk, RET_QK), lambda bi, c: (bi, c, 0)),
                  pl.BlockSpec((1, chunk, RET_QK), lambda bi, c: (bi, c, k_col)),
                  pl.BlockSpec((1, chunk, RET_V), lambda bi, c: (bi, c, v_col)),
                  pl.BlockSpec((1, chunk, RET_V), lambda bi, c: (bi, c, gate_col)),
                  pos_tab, pos_tab, pos_tab,
                  head_tab(RET_ROWS, RET_ROWS), head_tab(RET_ROWS, RET_DK), head_tab(RET_ROWS, RET_DK),
                  head_tab(1, RET_DV),
                  state((lambda bi, c: (bi, 0, 0, 0)) if has_state else (lambda bi, c: (0, 0, 0, 0)))],
        out_specs=[pl.BlockSpec((1, chunk, RET_V), lambda bi, c: (bi, c, 0)),
                   state(lambda bi, c: (bi, 0, 0, 0))],
        out_shape=[jax.ShapeDtypeStruct((b, l, RET_V), proj.dtype),
                   jax.ShapeDtypeStruct((b, RET_HEADS, RET_DK, RET_DV), F32)],
        compiler_params=_params("parallel", "arbitrary"),
        name="retention",
    )(proj, proj, proj, proj, *tables, s0)


def _moba_prompt_kernel(q_ref, k_ref, v_ref, o_ref, *, nb):
    k = k_ref[0]
    s_len = k.shape[0]
    means = jnp.sum(k.reshape(nb, MOBA_BLOCK, ATTN_DH), axis=1) * (1.0 / MOBA_BLOCK)
    means = _pad_rows(means, -(-nb // 8) * 8)
    blk = lax.broadcasted_iota(jnp.int32, (means.shape[0], MOBA_BLOCK), 0)
    key_lane = lax.broadcasted_iota(jnp.int32, (s_len, 128), 1)
    key_block = lax.broadcasted_iota(jnp.int32, (s_len, 128), 0) // MOBA_BLOCK
    kb = jnp.concatenate([k.astype(BF16), jnp.where(key_block == key_lane, 1.0, 0.0).astype(BF16)], axis=1)
    vb = jnp.concatenate([v_ref[0].astype(BF16), jnp.where(key_lane == 0, 1.0, 0.0).astype(BF16)], axis=1)
    row = lax.broadcasted_iota(jnp.int32, (MOBA_BLOCK, MOBA_BLOCK), 0)
    col = lax.broadcasted_iota(jnp.int32, (MOBA_BLOCK, MOBA_BLOCK), 1)
    causal = col <= row
    eye = jnp.where(col == row, 1.0, 0.0).astype(BF16)

    def masked_logits(i):
        q = q_ref[0, i * MOBA_BLOCK:(i + 1) * MOBA_BLOCK, :]
        bias = jnp.zeros((MOBA_BLOCK, 128), BF16)
        if i > 0:
            s = lax.dot_general(means, q, _NT, preferred_element_type=F32, precision=lax.Precision.HIGHEST)
            elig = blk < i
            sm = jnp.where(elig, s, NEG)
            rank = jnp.zeros(sm.shape, F32)
            for m in range(i):
                cm = sm[m:m + 1, :]
                beats = jnp.where(cm > sm, 1.0, jnp.where(cm == sm, jnp.where(blk > m, 1.0, 0.0), 0.0))
                rank = rank + beats
            keep = jnp.where(rank < MOBA_TOPK, jnp.where(sm > 0.5 * NEG, 0.0, NEG), NEG)
            bias_t = _pad_rows(jnp.where(elig, keep, 0.0), 128).astype(BF16)
            bias = lax.dot_general(eye, bias_t, _NT, preferred_element_type=F32).astype(BF16)
        qa = jnp.concatenate([(q * (ATTN_DH ** -0.5 * LOG2_E)).astype(BF16), bias], axis=1)
        kv_rows = (i + 1) * MOBA_BLOCK
        logits = lax.dot_general(qa, kb[:kv_rows], _NT, preferred_element_type=F32)
        own = jnp.where(causal, logits[:, i * MOBA_BLOCK:], NEG)
        return own if i == 0 else jnp.concatenate([logits[:, :i * MOBA_BLOCK], own], axis=1)

    def attend(i, logits):
        p = jnp.exp2(logits - jnp.max(logits, axis=-1, keepdims=True))
        acc = jnp.dot(p.astype(BF16), vb[:(i + 1) * MOBA_BLOCK], preferred_element_type=F32)
        o_ref[0, i * MOBA_BLOCK:(i + 1) * MOBA_BLOCK, :] = (
            acc[:, :ATTN_DH] / acc[:, ATTN_DH:ATTN_DH + 1]).astype(o_ref.dtype)

    ahead = 1
    pending = [masked_logits(i) for i in range(min(ahead, nb))]
    for i in range(nb):
        if i + ahead < nb:
            pending.append(masked_logits(i + ahead))
        attend(i, pending.pop(0))


def moba_prompt(q, k, v):
    b, s, _ = q.shape
    assert s % MOBA_BLOCK == 0
    nb = s // MOBA_BLOCK
    spec = pl.BlockSpec((1, s, ATTN_DH), lambda bi, h: (bi, 0, h))
    return pl.pallas_call(
        functools.partial(_moba_prompt_kernel, nb=nb),
        grid=(b, ATTN_HEADS),
        in_specs=[spec, spec, spec],
        out_specs=spec,
        out_shape=jax.ShapeDtypeStruct((b, s, ATTN_HEADS * ATTN_DH), BF16),
        compiler_params=_params("parallel", "parallel"),
        name="moba_prompt",
    )(q, k, v)


def _block_diag_queries(q):
    b, l, h, dh = q.shape
    eye = jnp.eye(h, dtype=q.dtype)
    w = q.transpose(0, 2, 1, 3)[:, :, :, None, :] * eye[None, :, None, :, None]
    return w.reshape(b, h * l, h * dh)


def _page_rows(page_ref):
    heads = [page_ref[pl.ds(h, PAGE_SIZE, stride=ATTN_HEADS), :] for h in range(ATTN_HEADS)]
    return jnp.concatenate(heads, axis=1).astype(BF16)


class _PageStream:
    def __init__(self, pt_ref, cache_ref, buf_ref, sem_ref, n_pages):
        self.pt_ref, self.cache_ref, self.buf_ref, self.sem_ref = pt_ref, cache_ref, buf_ref, sem_ref
        self.n_pages = n_pages
        self.total = pl.num_programs(0) * n_pages
        self.base = pl.program_id(0) * n_pages

    def _copy(self, page, g):
        slot = g % RING_PAGES
        return pltpu.make_async_copy(self.cache_ref.at[page], self.buf_ref.at[slot], self.sem_ref.at[slot])

    def _start(self, g):
        self._copy(self.pt_ref[g // self.n_pages, g % self.n_pages], g).start()

    def prime(self):
        @pl.when(pl.program_id(0) == 0)
        def _():
            for g in range(RING_PAGES - STEP_PAGES):
                self._start(g)

    def group(self, it):
        g0 = self.base + it * STEP_PAGES
        for j in range(STEP_PAGES):
            ahead = g0 + (RING_PAGES - STEP_PAGES) + j

            @pl.when(ahead < self.total)
            def _(ahead=ahead):
                self._start(ahead)

        pages = []
        for j in range(STEP_PAGES):
            self._copy(0, g0 + j).wait()
            pages.append(self.buf_ref.at[(g0 + j) % RING_PAGES])
        return pages


def _moba_sample_keys_kernel(pt_ref, wq_ref, ck_ref, bsum_ref, lg_ref, buf_ref, sem_ref, *, n_pages):
    stream = _PageStream(pt_ref, ck_ref, buf_ref, sem_ref, n_pages)
    stream.prime()
    w = wq_ref[0]

    def body(it, carry):
        pages = stream.group(it)
        sums = [jnp.sum(p[...].reshape(PAGE_SIZE, ATTN_HEADS, ATTN_DH), axis=0) for p in pages]
        keys = jnp.concatenate([_page_rows(p) for p in pages], axis=0)
        logits = lax.dot_general(w, keys, _NT, preferred_element_type=F32)
        for j in range(STEP_BLOCKS):
            blk = it * STEP_BLOCKS + j
            bsum_ref[0, blk] = functools.reduce(
                lambda x, y: x + y, sums[j * PAGES_PER_BLOCK:(j + 1) * PAGES_PER_BLOCK])
            lg_ref[0, blk] = logits[:, j * MOBA_BLOCK:(j + 1) * MOBA_BLOCK]
        return carry

    lax.fori_loop(0, n_pages // STEP_PAGES, body, 0)


def _moba_sample_values_kernel(pt_ref, wq_ref, wqs_ref, bsum_ref, kn_ref, vn_ref, lg_ref, cv_ref, o_ref,
                               buf_ref, sem_ref, sel_ref, m_ref, l_ref, acc_ref, *, n_new, n_past_blocks, n_pages):
    stream = _PageStream(pt_ref, cv_ref, buf_ref, sem_ref, n_pages)
    stream.prime()
    rows = wq_ref.shape[1]
    lane = lax.broadcasted_iota(jnp.int32, (rows, 128), 1)

    kn = kn_ref[0]
    first_row = lax.broadcasted_iota(jnp.int32, (8, kn.shape[1]), 0) == 0
    own_sum = jnp.where(first_row, jnp.sum(kn, axis=0, keepdims=True), 0.0)
    means = _pad_rows(jnp.concatenate([bsum_ref[0], own_sum], axis=0) * (1.0 / MOBA_BLOCK), 128)
    s = lax.dot_general(wq_ref[0], means, _NT, preferred_element_type=F32, precision=lax.Precision.HIGHEST)
    work = jnp.where(lane < n_past_blocks, s, NEG)
    sel = jnp.zeros(work.shape, F32)
    for _ in range(MOBA_TOPK):
        mx = jnp.max(work, axis=-1, keepdims=True)
        first = jnp.min(jnp.where(work == mx, lane, 128), axis=-1, keepdims=True)
        pick = lane == first
        sel = jnp.where(pick, jnp.where(mx > 0.5 * NEG, 1.0, 0.0), sel)
        work = jnp.where(pick, -jnp.inf, work)
    sel_ref[...] = sel
    t = lax.broadcasted_iota(jnp.int32, (rows, 128), 0) % n_new
    lo = lax.dot_general(wqs_ref[0], _pad_rows(kn, 128).astype(BF16), _NT, preferred_element_type=F32)
    lo = jnp.where(lane <= t, lo, NEG)
    m0 = jnp.max(lo, axis=-1, keepdims=True)
    p0 = jnp.exp(lo - m0)
    m_ref[...] = jnp.broadcast_to(m0, m_ref.shape)
    l_ref[...] = jnp.broadcast_to(jnp.sum(p0, axis=-1, keepdims=True), l_ref.shape)
    acc_ref[...] = jnp.dot(p0.astype(BF16), _pad_rows(vn_ref[0], 128).astype(BF16), preferred_element_type=F32)

    def body(it, carry):
        pages = stream.group(it)
        sel = sel_ref[...]
        lgs = []
        for j in range(STEP_BLOCKS):
            blk = it * STEP_BLOCKS + j
            picked = jnp.max(jnp.where(lane == blk, sel, 0.0), axis=-1, keepdims=True) > 0.5
            lgs.append(jnp.where(picked, lg_ref[0, blk], NEG))
        m_old = m_ref[:, :1]
        m_new = functools.reduce(jnp.maximum, [m_old] + [jnp.max(lg, axis=-1, keepdims=True) for lg in lgs])
        alpha = jnp.exp(m_old - m_new)
        ps = [jnp.exp(lg - m_new) for lg in lgs]
        m_ref[...] = jnp.broadcast_to(m_new, m_ref.shape)
        l_ref[...] = alpha * l_ref[...] + functools.reduce(
            lambda x, y: x + y, [jnp.sum(p, axis=-1, keepdims=True) for p in ps])
        values = jnp.concatenate([_page_rows(p) for p in pages], axis=0)
        acc_ref[...] = alpha * acc_ref[...] + jnp.dot(
            jnp.concatenate(ps, axis=1).astype(BF16), values, preferred_element_type=F32)
        return carry

    lax.fori_loop(0, n_pages // STEP_PAGES, body, 0)

    out = acc_ref[...] / l_ref[:, :1]
    for h in range(ATTN_HEADS):
        o_ref[0, :, h * ATTN_DH:(h + 1) * ATTN_DH] = (
            out[h * n_new:(h + 1) * n_new, h * ATTN_DH:(h + 1) * ATTN_DH].astype(o_ref.dtype))


def moba_sample(q, k_new, v_new, cache_k, cache_v, page_table):
    b, l, d = q.shape
    n_pages = PAST_LEN // PAGE_SIZE
    assert n_pages % STEP_PAGES == 0 and (PAST_LEN + l - 1) // MOBA_BLOCK == PAST_LEN // MOBA_BLOCK
    nblk = n_pages // PAGES_PER_BLOCK
    rows = ATTN_HEADS * l
    page_rows = PAGE_SIZE * ATTN_HEADS
    ck = cache_k.reshape(-1, page_rows, ATTN_DH)
    cv = cache_v.reshape(-1, page_rows, ATTN_DH)
    wq = _block_diag_queries(q.reshape(b, l, ATTN_HEADS, ATTN_DH))
    wqs = (wq * ATTN_DH ** -0.5).astype(BF16)

    per_batch = lambda *shape: pl.BlockSpec((1,) + shape, lambda bi, pt: (bi,) + (0,) * len(shape))
    cache = pl.BlockSpec(memory_space=pl.ANY)
    ring = [pltpu.VMEM((RING_PAGES, page_rows, ATTN_DH), F32), pltpu.SemaphoreType.DMA((RING_PAGES,))]

    bsum, logits = pl.pallas_call(
        functools.partial(_moba_sample_keys_kernel, n_pages=n_pages),
        grid_spec=pltpu.PrefetchScalarGridSpec(
            num_scalar_prefetch=1,
            grid=(b,),
            in_specs=[per_batch(rows, d), cache],
            out_specs=[per_batch(nblk, ATTN_HEADS, ATTN_DH), per_batch(nblk, rows, MOBA_BLOCK)],
            scratch_shapes=ring),
        out_shape=[jax.ShapeDtypeStruct((b, nblk, ATTN_HEADS, ATTN_DH), F32),
                   jax.ShapeDtypeStruct((b, nblk, rows, MOBA_BLOCK), F32)],
        compiler_params=_params("arbitrary"),
        name="moba_sample_keys",
    )(page_table, wqs, ck)

    return pl.pallas_call(
        functools.partial(_moba_sample_values_kernel, n_new=l, n_past_blocks=nblk, n_pages=n_pages),
        grid_spec=pltpu.PrefetchScalarGridSpec(
            num_scalar_prefetch=1,
            grid=(b,),
            in_specs=[per_batch(rows, d), per_batch(rows, d), per_batch(nblk, d), per_batch(l, d), per_batch(l, d),
                      per_batch(nblk, rows, MOBA_BLOCK), cache],
            out_specs=per_batch(l, d),
            scratch_shapes=ring + [pltpu.VMEM((rows, 128), F32), pltpu.VMEM((rows, 128), F32),
                                   pltpu.VMEM((rows, 128), F32), pltpu.VMEM((rows, d), F32)]),
        out_shape=jax.ShapeDtypeStruct((b, l, d), F32),
        compiler_params=_params("arbitrary"),
        name="moba_sample_values",
    )(page_table, wq, wqs, bsum.reshape(b, nblk, d), k_new, v_new, logits, cv)


def kernel(x_prompt, x_sample, state_ret, cache_k, cache_v, page_table, ret_norm_g, ret_w_in, ret_w_out,
           attn_norm_g, attn_w_q, attn_w_out, kv_norm_g, w_kv, mlp_norm_g, mlp_w_up, mlp_w_down, final_norm_g):
    bp, lp, d = x_prompt.shape
    bs, ls, _ = x_sample.shape
    assert ret_w_in.shape[0] == 1 and attn_w_q.shape[0] == 1 and mlp_w_up.shape[0] == 2

    def trunk(x, pos, s0, attend):
        b, l, _ = x.shape
        h = x.reshape(b * l, d)
        act = BF16 if l % RET_ROWS == 0 else F32
        proj = norm_matmul(h, ret_norm_g[0], ret_w_in[0], act)
        o, s = retention(proj.reshape(b, l, -1), pos, s0)
        h = matmul_residual(o.reshape(b * l, RET_V).astype(BF16), ret_w_out[0], h)
        h = mlp(h, mlp_norm_g[0], mlp_w_up, mlp_w_down, 0)
        k, v, q = kvq_proj(h, kv_norm_g, attn_norm_g[0], w_kv, attn_w_q[0])
        o = attend(q.reshape(b, l, d), k.reshape(b, l, d), v.reshape(b, l, d))
        h = matmul_residual(o.reshape(b * l, d).astype(BF16), attn_w_out[0], h)
        y = mlp(h, mlp_norm_g[1], mlp_w_up, mlp_w_down, 1, g_final=final_norm_g)
        kv_shape = (b, l, ATTN_HEADS, ATTN_DH)
        return y.reshape(b, l, d), s[None], k.reshape(kv_shape), v.reshape(kv_shape)

    y_p, s_p, k_p, v_p = trunk(x_prompt, jnp.arange(lp, dtype=jnp.int32), None, moba_prompt)
    y_s, s_s, k_s, v_s = trunk(
        x_sample, PAST_LEN + jnp.arange(ls, dtype=jnp.int32), state_ret[0],
        lambda q, k, v: moba_sample(q, k, v, cache_k, cache_v, page_table))
    return (y_p, y_s, s_p, s_s, k_p, v_p, k_s, v_s)
```

```python
import functools
import math

import jax
import jax.numpy as jnp
from jax import lax
from jax.experimental import pallas as pl
from jax.experimental.pallas import tpu as pltpu

F32 = jnp.float32
BF16 = jnp.bfloat16

D_MODEL = 1024
PAST_LEN = 8192
PAGE_SIZE = 128
RET_HEADS = 4
RET_DK = D_MODEL // RET_HEADS
RET_QK = RET_HEADS * RET_DK
RET_V = 2 * D_MODEL
RET_DV = RET_V // RET_HEADS
RET_CHUNK = 128
ATTN_HEADS = 8
ATTN_DH = D_MODEL // ATTN_HEADS
MOBA_BLOCK = 256
MOBA_TOPK = 3
D_FF = 4 * D_MODEL
NORM_EPS = 1e-5
GN_EPS = 1e-6
NEG = -1e30
LOG2_E = math.log2(math.e)

PAGES_PER_BLOCK = MOBA_BLOCK // PAGE_SIZE
STEP_PAGES = 8
STEP_BLOCKS = STEP_PAGES // PAGES_PER_BLOCK
RING_PAGES = 3 * STEP_PAGES
ROW_CHUNKS = 4
RET_ROWS = 128
VMEM_LIMIT_BYTES = 56 * 1024 * 1024

_NT = (((1,), (1,)), ((), ()))
_TN = (((0,), (0,)), ((), ()))


def _params(*sem):
    return pltpu.CompilerParams(dimension_semantics=sem, vmem_limit_bytes=VMEM_LIMIT_BYTES)


def _rms(x, g, eps=NORM_EPS):
    return x * lax.rsqrt(jnp.mean(x * x, axis=-1, keepdims=True) + eps) * g


def _row_chunks(rows):
    size = rows // ROW_CHUNKS if rows % (ROW_CHUNKS * 16) == 0 else rows
    return [slice(r, r + size) for r in range(0, rows, size)]


def _norm_matmul_kernel(x_ref, g_ref, w_ref, o_ref, wb_ref):
    @pl.when(pl.program_id(1) == 0)
    def _():
        wb_ref[...] = w_ref[...].astype(BF16)

    for rows in _row_chunks(x_ref.shape[0]):
        xn = _rms(x_ref[rows, :], g_ref[...]).astype(BF16)
        o_ref[rows, :] = jnp.dot(xn, wb_ref[...], preferred_element_type=F32).astype(o_ref.dtype)


def norm_matmul(x, g, w, out_dtype, tm=1024, tn=2048):
    m, d = x.shape
    n = w.shape[1]
    tm = min(tm, m)
    return pl.pallas_call(
        _norm_matmul_kernel,
        grid=(n // tn, m // tm),
        in_specs=[pl.BlockSpec((tm, d), lambda j, i: (i, 0)),
                  pl.BlockSpec((1, d), lambda j, i: (0, 0)),
                  pl.BlockSpec((d, tn), lambda j, i: (0, j))],
        out_specs=pl.BlockSpec((tm, tn), lambda j, i: (i, j)),
        out_shape=jax.ShapeDtypeStruct((m, n), out_dtype),
        scratch_shapes=[pltpu.VMEM((d, tn), BF16)],
        compiler_params=_params("parallel", "arbitrary"),
        name="norm_matmul",
    )(x, g.reshape(1, d), w)


def _resident_weight(shape):
    return pl.BlockSpec(shape, lambda i: (0,) * len(shape), pipeline_mode=pl.Buffered(1))


def _matmul_residual_kernel(a_ref, w_ref, r_ref, o_ref, wb_ref):
    @pl.when(pl.program_id(0) == 0)
    def _():
        wb_ref[...] = w_ref[...].astype(BF16)

    o_ref[...] = r_ref[...] + jnp.dot(a_ref[...], wb_ref[...], preferred_element_type=F32)


def matmul_residual(a, w, res, tm=1024):
    m, k = a.shape
    n = w.shape[1]
    tm = min(tm, m)
    return pl.pallas_call(
        _matmul_residual_kernel,
        grid=(m // tm,),
        in_specs=[pl.BlockSpec((tm, k), lambda i: (i, 0)),
                  _resident_weight((k, n)),
                  pl.BlockSpec((tm, n), lambda i: (i, 0))],
        out_specs=pl.BlockSpec((tm, n), lambda i: (i, 0)),
        out_shape=jax.ShapeDtypeStruct((m, n), F32),
        scratch_shapes=[pltpu.VMEM((k, n), BF16)],
        compiler_params=_params("arbitrary"),
        name="matmul_residual",
    )(a, w, res)


def _mlp_kernel(x_ref, g_ref, wu_ref, wd_ref, gf_ref, o_ref, xn_ref, acc_ref, *, final_norm):
    f = pl.program_id(1)
    last = pl.num_programs(1) - 1

    def weights():
        return wu_ref[...].astype(BF16), wd_ref[...].astype(BF16)

    def hidden(xn, wu, wd):
        u = jnp.maximum(jnp.dot(xn, wu, preferred_element_type=F32), 0.0)
        return jnp.dot((u * u).astype(BF16), wd, preferred_element_type=F32)

    @pl.when(f == 0)
    def _():
        wu, wd = weights()
        for rows in _row_chunks(x_ref.shape[0]):
            xn = _rms(x_ref[rows, :], g_ref[...]).astype(BF16)
            xn_ref[rows, :] = xn
            acc_ref[rows, :] = hidden(xn, wu, wd)

    @pl.when(jnp.logical_and(f != 0, f != last))
    def _():
        acc_ref[...] += hidden(xn_ref[...], *weights())

    @pl.when(f == last)
    def _():
        wu, wd = weights()
        for rows in _row_chunks(x_ref.shape[0]):
            h = x_ref[rows, :] + acc_ref[rows, :] + hidden(xn_ref[rows, :], wu, wd)
            o_ref[rows, :] = _rms(h, gf_ref[...]) if final_norm else h


def mlp(x, g, w_up, w_down, layer, g_final=None, tm=1024, tf=1024):
    m, d = x.shape
    ff = w_up.shape[2]
    assert ff // tf >= 2
    tm = min(tm, m)
    final_norm = g_final is not None
    gf = g_final if final_norm else g
    return pl.pallas_call(
        functools.partial(_mlp_kernel, final_norm=final_norm),
        grid=(m // tm, ff // tf),
        in_specs=[pl.BlockSpec((tm, d), lambda i, f: (i, 0)),
                  pl.BlockSpec((1, d), lambda i, f: (0, 0)),
                  pl.BlockSpec((None, d, tf), lambda i, f: (layer, 0, f)),
                  pl.BlockSpec((None, tf, d), lambda i, f: (layer, f, 0)),
                  pl.BlockSpec((1, d), lambda i, f: (0, 0))],
        out_specs=pl.BlockSpec((tm, d), lambda i, f: (i, 0)),
        out_shape=jax.ShapeDtypeStruct((m, d), F32),
        scratch_shapes=[pltpu.VMEM((tm, d), BF16), pltpu.VMEM((tm, d), F32)],
        compiler_params=_params("parallel", "arbitrary"),
        name="mlp",
    )(x, g.reshape(1, d), w_up, w_down, gf.reshape(1, d))


def _kvq_kernel(x_ref, gkv_ref, gq_ref, wkv_ref, wq_ref, k_ref, v_ref, q_ref, wkvb_ref, wqb_ref):
    @pl.when(pl.program_id(0) == 0)
    def _():
        wkvb_ref[...] = wkv_ref[...].astype(BF16)
        wqb_ref[...] = wq_ref[...].astype(BF16)

    d = x_ref.shape[1]
    x = x_ref[...]
    r = x * lax.rsqrt(jnp.mean(x * x, axis=-1, keepdims=True) + NORM_EPS)
    xkv = (r * gkv_ref[...]).astype(BF16)
    xq = (r * gq_ref[...]).astype(BF16)
    k_ref[...] = jnp.dot(xkv, wkvb_ref[:, :d], preferred_element_type=F32)
    v_ref[...] = jnp.dot(xkv, wkvb_ref[:, d:], preferred_element_type=F32)
    q_ref[...] = jnp.dot(xq, wqb_ref[...], preferred_element_type=F32)


def kvq_proj(x, g_kv, g_q, w_kv, w_q, tm=512):
    m, d = x.shape
    tm = min(tm, m)
    row = pl.BlockSpec((tm, d), lambda i: (i, 0))
    gain = pl.BlockSpec((1, d), lambda i: (0, 0))
    out = jax.ShapeDtypeStruct((m, d), F32)
    return pl.pallas_call(
        _kvq_kernel,
        grid=(m // tm,),
        in_specs=[row, gain, gain, _resident_weight((d, 2 * d)), _resident_weight((d, d))],
        out_specs=[row, row, row],
        out_shape=[out, out, out],
        scratch_shapes=[pltpu.VMEM((d, 2 * d), BF16), pltpu.VMEM((d, d), BF16)],
        compiler_params=_params("arbitrary"),
        name="kvq_proj",
    )(x, g_kv.reshape(1, d), g_q.reshape(1, d), w_kv, w_q)


def _ret_log_decay():
    return jnp.log(1.0 - 2.0 ** (-5.0 - jnp.arange(RET_HEADS, dtype=F32)))


def _ret_tables(pos, chunk):
    angle = 1.0 / (10000.0 ** jnp.linspace(0.0, 1.0, RET_DK // 2, dtype=F32))
    angle = jnp.repeat(angle, 2)
    ang = pos.astype(F32)[:, None] * angle[None, :]
    sin, cos = jnp.sin(ang), jnp.cos(ang)
    even = (jnp.arange(RET_DK) % 2 == 0)[None, :]
    sin_next = jnp.where(even, -sin, 0.0)
    sin_prev = jnp.where(even, 0.0, sin)

    lg = _ret_log_decay()
    i = jnp.arange(chunk, dtype=F32)
    diff = i[:, None] - i[None, :]
    d_intra = jnp.exp(jnp.where(diff >= 0, lg[:, None, None] * diff, -jnp.inf))
    q_dec = jnp.exp(lg[:, None] * (i + 1.0))
    k_dec = jnp.exp(lg[:, None] * (chunk - 1.0 - i))
    c_dec = jnp.exp(lg * chunk)
    d_intra = d_intra * RET_DK ** -0.5
    k_dec = k_dec * RET_DK ** -0.5
    pad = RET_ROWS - chunk
    d_intra = jnp.pad(d_intra, ((0, 0), (0, pad), (0, pad)))
    q_dec = jnp.broadcast_to(jnp.pad(q_dec, ((0, 0), (0, pad)))[:, :, None], (RET_HEADS, RET_ROWS, RET_DK))
    k_dec = jnp.broadcast_to(jnp.pad(k_dec, ((0, 0), (0, pad)))[:, :, None], (RET_HEADS, RET_ROWS, RET_DK))
    c_dec = jnp.broadcast_to(c_dec[:, None, None], (RET_HEADS, 1, RET_DV))
    return cos, sin_next, sin_prev, d_intra, q_dec, k_dec, c_dec


def _theta_shift(x, cos, sin_next, sin_prev):
    dk = x.shape[-1]
    return x * cos + pltpu.roll(x, dk - 1, 1) * sin_next + pltpu.roll(x, 1, 1) * sin_prev


def _pad_rows(x, rows):
    if x.shape[0] == rows:
        return x
    return jnp.concatenate([x, jnp.zeros((rows - x.shape[0], x.shape[1]), x.dtype)], axis=0)


def _retention_kernel(q_ref, k_ref, v_ref, gate_ref, cos_ref, sn_ref, sp_ref, dintra_ref, qdec_ref, kdec_ref,
                      cdec_ref, s0_ref, o_ref, sout_ref, *stream, chunk, n_chunks, n_batch, has_state):
    seeded = n_chunks > 1
    streamed = bool(stream)
    if seeded:
        @pl.when(pl.program_id(1) == 0)
        def _():
            sout_ref[0] = s0_ref[0] if has_state else jnp.zeros(sout_ref.shape[1:], F32)
    if streamed:
        sin_buf, sout_buf, sem_in, sem_out = stream
        b = pl.program_id(0)
        slot = b % 2

        def fetch(bb, sl, h):
            return pltpu.make_async_copy(s0_ref.at[bb, h], sin_buf.at[sl, h], sem_in.at[sl, h])

        def flush(bb, sl, h):
            return pltpu.make_async_copy(sout_buf.at[sl, h], sout_ref.at[bb, h], sem_out.at[sl, h])

        @pl.when(b == 0)
        def _():
            for h in range(RET_HEADS):
                fetch(0, 0, h).start()

        @pl.when(b + 1 < n_batch)
        def _():
            for h in range(RET_HEADS):
                fetch(b + 1, 1 - slot, h).start()

        @pl.when(b >= 2)
        def _():
            for h in range(RET_HEADS):
                flush(b - 2, slot, h).wait()

    cos, sn, sp = cos_ref[...], sn_ref[...], sp_ref[...]

    def scores(h):
        qk_cols = slice(h * RET_DK, (h + 1) * RET_DK)
        q = _theta_shift(q_ref[0, :, qk_cols].astype(F32), cos, sn, sp)
        k = _theta_shift(k_ref[0, :, qk_cols].astype(F32), cos, sn, sp)
        q = _pad_rows(q, RET_ROWS)
        k = _pad_rows(k, RET_ROWS)
        a = lax.dot_general(q.astype(BF16), k.astype(BF16), _NT, preferred_element_type=F32) * dintra_ref[h]
        return q, k, a

    def finish(h, q, k, a):
        v_cols = slice(h * RET_DV, (h + 1) * RET_DV)
        v = _pad_rows(v_ref[0, :, v_cols], RET_ROWS).astype(BF16)
        if streamed:
            fetch(b, slot, h).wait()
            s = sin_buf[slot, h]
        elif seeded:
            s = sout_ref[0, h]
        else:
            s = s0_ref[0, h] if has_state else jnp.zeros((RET_DK, RET_DV), F32)
        o = (jnp.dot(a.astype(BF16), v, preferred_element_type=F32)
             + jnp.dot((q * qdec_ref[h]).astype(BF16), s.astype(BF16), preferred_element_type=F32))
        s_new = s * cdec_ref[h] + lax.dot_general((k * kdec_ref[h]).astype(BF16), v, _TN,
                                                  preferred_element_type=F32)
        if streamed:
            sout_buf[slot, h] = s_new
            flush(b, slot, h).start()
        else:
            sout_ref[0, h] = s_new
        o = o[:chunk]
        o = o * lax.rsqrt(jnp.mean(o * o, axis=-1, keepdims=True) + GN_EPS)
        gate = gate_ref[0, :, v_cols].astype(F32)
        o_ref[0, :, v_cols] = (o * (gate * jax.nn.sigmoid(gate))).astype(o_ref.dtype)

    pending = scores(0)
    for h in range(RET_HEADS):
        following = scores(h + 1) if h + 1 < RET_HEADS else None
        finish(h, *pending)
        pending = following

    if streamed:
        @pl.when(b == n_batch - 1)
        def _():
            for h in range(RET_HEADS):
                if n_batch > 1:
                    flush(b - 1, 1 - slot, h).wait()
                flush(b, slot, h).wait()


def retention(proj, pos, s0):
    b, l, _ = proj.shape
    chunk = math.gcd(l, RET_CHUNK)
    n = l // chunk
    has_state = s0 is not None
    if not has_state:
        s0 = jnp.zeros((1, RET_HEADS, RET_DK, RET_DV), F32)
    tables = _ret_tables(pos, chunk)
    k_col, v_col, gate_col = 1, 2 * RET_QK // RET_V, 2 * RET_QK // RET_V + 1
    pos_tab = pl.BlockSpec((chunk, RET_DK), lambda bi, c: (c, 0))
    head_tab = lambda rows, cols: pl.BlockSpec((RET_HEADS, rows, cols), lambda bi, c: (0, 0, 0))
    state = lambda index_map: pl.BlockSpec((1, RET_HEADS, RET_DK, RET_DV), index_map)
    streamed = has_state and n == 1
    if streamed:
        state_in = state_out = pl.BlockSpec(memory_space=pl.ANY)
        slots = (2, RET_HEADS, RET_DK, RET_DV)
        scratch = [pltpu.VMEM(slots, F32), pltpu.VMEM(slots, F32),
                   pltpu.SemaphoreType.DMA(slots[:2]), pltpu.SemaphoreType.DMA(slots[:2])]
        semantics = ("arbitrary", "arbitrary")
    else:
        state_in = state((lambda bi, c: (bi, 0, 0, 0)) if has_state else (lambda bi, c: (0, 0, 0, 0)))
        state_out = state(lambda bi, c: (bi, 0, 0, 0))
        scratch = []
        semantics = ("parallel", "arbitrary")
    return pl.pallas_call(
        functools.partial(_retention_kernel, chunk=chunk, n_chunks=n, n_batch=b, has_state=has_state),
        grid=(b, n),
        in_specs=[pl.BlockSpec((1, chunk, RET_QK), lambda bi, c: (bi, c, 0)),
                  pl.BlockSpec((1, chunk, RET_QK), lambda bi, c: (bi, c, k_col)),
                  pl.BlockSpec((1, chunk, RET_V), lambda bi, c: (bi, c, v_col)),
                  pl.BlockSpec((1, chunk, RET_V), lambda bi, c: (bi, c, gate_col)),
                  pos_tab, pos_tab, pos_tab,
                  head_tab(RET_ROWS, RET_ROWS), head_tab(RET_ROWS, RET_DK), head_tab(RET_ROWS, RET_DK),
                  head_tab(1, RET_DV), state_in],
        out_specs=[pl.BlockSpec((1, chunk, RET_V), lambda bi, c: (bi, c, 0)), state_out],
        out_shape=[jax.ShapeDtypeStruct((b, l, RET_V), proj.dtype),
                   jax.ShapeDtypeStruct((b, RET_HEADS, RET_DK, RET_DV), F32)],
        scratch_shapes=scratch,
        compiler_params=_params(*semantics),
        name="retention",
    )(proj, proj, proj, proj, *tables, s0)


def _moba_prompt_kernel(q_ref, k_ref, v_ref, o_ref, *, nb):
    k = k_ref[0]
    s_len = k.shape[0]
    means = jnp.sum(k.reshape(nb, MOBA_BLOCK, ATTN_DH), axis=1) * (1.0 / MOBA_BLOCK)
    means = _pad_rows(means, -(-nb // 8) * 8)
    blk = lax.broadcasted_iota(jnp.int32, (means.shape[0], MOBA_BLOCK), 0)
    key_lane = lax.broadcasted_iota(jnp.int32, (s_len, 128), 1)
    key_block = lax.broadcasted_iota(jnp.int32, (s_len, 128), 0) // MOBA_BLOCK
    kb = jnp.concatenate([k.astype(BF16), jnp.where(key_block == key_lane, 1.0, 0.0).astype(BF16)], axis=1)
    vb = jnp.concatenate([v_ref[0].astype(BF16), jnp.where(key_lane == 0, 1.0, 0.0).astype(BF16)], axis=1)
    row = lax.broadcasted_iota(jnp.int32, (MOBA_BLOCK, MOBA_BLOCK), 0)
    col = lax.broadcasted_iota(jnp.int32, (MOBA_BLOCK, MOBA_BLOCK), 1)
    causal = col <= row
    eye = jnp.where(col == row, 1.0, 0.0).astype(BF16)

    def masked_logits(i):
        q = q_ref[0, i * MOBA_BLOCK:(i + 1) * MOBA_BLOCK, :]
        bias = jnp.zeros((MOBA_BLOCK, 128), BF16)
        if i > 0:
            s = lax.dot_general(means, q, _NT, preferred_element_type=F32, precision=lax.Precision.HIGHEST)
            elig = blk < i
            sm = jnp.where(elig, s, NEG)
            rank = jnp.zeros(sm.shape, F32)
            for m in range(i):
                cm = sm[m:m + 1, :]
                beats = jnp.where(cm > sm, 1.0, jnp.where(cm == sm, jnp.where(blk > m, 1.0, 0.0), 0.0))
                rank = rank + beats
            keep = jnp.where(rank < MOBA_TOPK, jnp.where(sm > 0.5 * NEG, 0.0, NEG), NEG)
            bias_t = _pad_rows(jnp.where(elig, keep, 0.0), 128).astype(BF16)
            bias = lax.dot_general(eye, bias_t, _NT, preferred_element_type=F32).astype(BF16)
        qa = jnp.concatenate([(q * (ATTN_DH ** -0.5 * LOG2_E)).astype(BF16), bias], axis=1)
        kv_rows = (i + 1) * MOBA_BLOCK
        logits = lax.dot_general(qa, kb[:kv_rows], _NT, preferred_element_type=F32)
        own = jnp.where(causal, logits[:, i * MOBA_BLOCK:], NEG)
        return own if i == 0 else jnp.concatenate([logits[:, :i * MOBA_BLOCK], own], axis=1)

    def attend(i, logits):
        p = jnp.exp2(logits - jnp.max(logits, axis=-1, keepdims=True))
        acc = jnp.dot(p.astype(BF16), vb[:(i + 1) * MOBA_BLOCK], preferred_element_type=F32)
        o_ref[0, i * MOBA_BLOCK:(i + 1) * MOBA_BLOCK, :] = (
            acc[:, :ATTN_DH] / acc[:, ATTN_DH:ATTN_DH + 1]).astype(o_ref.dtype)

    ahead = 1
    pending = [masked_logits(i) for i in range(min(ahead, nb))]
    for i in range(nb):
        if i + ahead < nb:
            pending.append(masked_logits(i + ahead))
        attend(i, pending.pop(0))


def moba_prompt(q, k, v):
    b, s, _ = q.shape
    assert s % MOBA_BLOCK == 0
    nb = s // MOBA_BLOCK
    spec = pl.BlockSpec((1, s, ATTN_DH), lambda bi, h: (bi, 0, h))
    return pl.pallas_call(
        functools.partial(_moba_prompt_kernel, nb=nb),
        grid=(b, ATTN_HEADS),
        in_specs=[spec, spec, spec],
        out_specs=spec,
        out_shape=jax.ShapeDtypeStruct((b, s, ATTN_HEADS * ATTN_DH), BF16),
        compiler_params=_params("parallel", "parallel"),
        name="moba_prompt",
    )(q, k, v)


def _block_diag_queries(q):
    b, l, h, dh = q.shape
    eye = jnp.eye(h, dtype=q.dtype)
    w = q.transpose(0, 2, 1, 3)[:, :, :, None, :] * eye[None, :, None, :, None]
    return w.reshape(b, h * l, h * dh)


def _page_rows(page_ref):
    heads = [page_ref[pl.ds(h, PAGE_SIZE, stride=ATTN_HEADS), :] for h in range(ATTN_HEADS)]
    return jnp.concatenate(heads, axis=1).astype(BF16)


class _PageStream:
    def __init__(self, pt_ref, cache_ref, buf_ref, sem_ref, n_pages):
        self.pt_ref, self.cache_ref, self.buf_ref, self.sem_ref = pt_ref, cache_ref, buf_ref, sem_ref
        self.n_pages = n_pages
        self.total = pl.num_programs(0) * n_pages
        self.base = pl.program_id(0) * n_pages

    def _copy(self, page, g):
        slot = g % RING_PAGES
        return pltpu.make_async_copy(self.cache_ref.at[page], self.buf_ref.at[slot], self.sem_ref.at[slot])

    def _start(self, g):
        self._copy(self.pt_ref[g // self.n_pages, g % self.n_pages], g).start()

    def prime(self):
        @pl.when(pl.program_id(0) == 0)
        def _():
            for g in range(RING_PAGES - STEP_PAGES):
                self._start(g)

    def group(self, it):
        g0 = self.base + it * STEP_PAGES
        for j in range(STEP_PAGES):
            ahead = g0 + (RING_PAGES - STEP_PAGES) + j

            @pl.when(ahead < self.total)
            def _(ahead=ahead):
                self._start(ahead)

        pages = []
        for j in range(STEP_PAGES):
            self._copy(0, g0 + j).wait()
            pages.append(self.buf_ref.at[(g0 + j) % RING_PAGES])
        return pages


def _moba_sample_keys_kernel(pt_ref, wq_ref, ck_ref, bsum_ref, lg_ref, buf_ref, sem_ref, *, n_pages):
    stream = _PageStream(pt_ref, ck_ref, buf_ref, sem_ref, n_pages)
    stream.prime()
    w = wq_ref[0]

    def body(it, carry):
        pages = stream.group(it)
        sums = [jnp.sum(p[...].reshape(PAGE_SIZE, ATTN_HEADS, ATTN_DH), axis=0) for p in pages]
        keys = jnp.concatenate([_page_rows(p) for p in pages], axis=0)
        logits = lax.dot_general(w, keys, _NT, preferred_element_type=F32)
        for j in range(STEP_BLOCKS):
            blk = it * STEP_BLOCKS + j
            bsum_ref[0, blk] = functools.reduce(
                lambda x, y: x + y, sums[j * PAGES_PER_BLOCK:(j + 1) * PAGES_PER_BLOCK])
            lg_ref[0, blk] = logits[:, j * MOBA_BLOCK:(j + 1) * MOBA_BLOCK]
        return carry

    lax.fori_loop(0, n_pages // STEP_PAGES, body, 0)


def _moba_sample_values_kernel(pt_ref, wq_ref, wqs_ref, bsum_ref, kn_ref, vn_ref, lg_ref, cv_ref, o_ref,
                               buf_ref, sem_ref, sel_ref, m_ref, l_ref, acc_ref, *, n_new, n_past_blocks, n_pages):
    stream = _PageStream(pt_ref, cv_ref, buf_ref, sem_ref, n_pages)
    stream.prime()
    rows = wq_ref.shape[1]
    lane = lax.broadcasted_iota(jnp.int32, (rows, 128), 1)

    kn = kn_ref[0]
    first_row = lax.broadcasted_iota(jnp.int32, (8, kn.shape[1]), 0) == 0
    own_sum = jnp.where(first_row, jnp.sum(kn, axis=0, keepdims=True), 0.0)
    means = _pad_rows(jnp.concatenate([bsum_ref[0], own_sum], axis=0) * (1.0 / MOBA_BLOCK), 128)
    s = lax.dot_general(wq_ref[0], means, _NT, preferred_element_type=F32, precision=lax.Precision.HIGHEST)
    work = jnp.where(lane < n_past_blocks, s, NEG)
    sel = jnp.zeros(work.shape, F32)
    for _ in range(MOBA_TOPK):
        mx = jnp.max(work, axis=-1, keepdims=True)
        first = jnp.min(jnp.where(work == mx, lane, 128), axis=-1, keepdims=True)
        pick = lane == first
        sel = jnp.where(pick, jnp.where(mx > 0.5 * NEG, 1.0, 0.0), sel)
        work = jnp.where(pick, -jnp.inf, work)
    sel_ref[...] = sel
    t = lax.broadcasted_iota(jnp.int32, (rows, 128), 0) % n_new
    lo = lax.dot_general(wqs_ref[0], _pad_rows(kn, 128).astype(BF16), _NT, preferred_element_type=F32)
    lo = jnp.where(lane <= t, lo, NEG)
    m0 = jnp.max(lo, axis=-1, keepdims=True)
    p0 = jnp.exp(lo - m0)
    m_ref[...] = jnp.broadcast_to(m0, m_ref.shape)
    l_ref[...] = jnp.broadcast_to(jnp.sum(p0, axis=-1, keepdims=True), l_ref.shape)
    acc_ref[...] = jnp.dot(p0.astype(BF16), _pad_rows(vn_ref[0], 128).astype(BF16), preferred_element_type=F32)

    def body(it, carry):
        pages = stream.group(it)
        sel = sel_ref[...]
        lgs = []
        for j in range(STEP_BLOCKS):
            blk = it * STEP_BLOCKS + j
            picked = jnp.max(jnp.where(lane == blk, sel, 0.0), axis=-1, keepdims=True) > 0.5
            lgs.append(jnp.where(picked, lg_ref[0, blk], NEG))
        m_old = m_ref[:, :1]
        m_new = functools.reduce(jnp.maximum, [m_old] + [jnp.max(lg, axis=-1, keepdims=True) for lg in lgs])
        alpha = jnp.exp(m_old - m_new)
        ps = [jnp.exp(lg - m_new) for lg in lgs]
        m_ref[...] = jnp.broadcast_to(m_new, m_ref.shape)
        l_ref[...] = alpha * l_ref[...] + functools.reduce(
            lambda x, y: x + y, [jnp.sum(p, axis=-1, keepdims=True) for p in ps])
        values = jnp.concatenate([_page_rows(p) for p in pages], axis=0)
        acc_ref[...] = alpha * acc_ref[...] + jnp.dot(
            jnp.concatenate(ps, axis=1).astype(BF16), values, preferred_element_type=F32)
        return carry

    lax.fori_loop(0, n_pages // STEP_PAGES, body, 0)

    out = acc_ref[...] / l_ref[:, :1]
    for h in range(ATTN_HEADS):
        o_ref[0, :, h * ATTN_DH:(h + 1) * ATTN_DH] = (
            out[h * n_new:(h + 1) * n_new, h * ATTN_DH:(h + 1) * ATTN_DH].astype(o_ref.dtype))


def moba_sample(q, k_new, v_new, cache_k, cache_v, page_table):
    b, l, d = q.shape
    n_pages = PAST_LEN // PAGE_SIZE
    assert n_pages % STEP_PAGES == 0 and (PAST_LEN + l - 1) // MOBA_BLOCK == PAST_LEN // MOBA_BLOCK
    nblk = n_pages // PAGES_PER_BLOCK
    rows = ATTN_HEADS * l
    page_rows = PAGE_SIZE * ATTN_HEADS
    ck = cache_k.reshape(-1, page_rows, ATTN_DH)
    cv = cache_v.reshape(-1, page_rows, ATTN_DH)
    wq = _block_diag_queries(q.reshape(b, l, ATTN_HEADS, ATTN_DH))
    wqs = (wq * ATTN_DH ** -0.5).astype(BF16)

    per_batch = lambda *shape: pl.BlockSpec((1,) + shape, lambda bi, pt: (bi,) + (0,) * len(shape))
    cache = pl.BlockSpec(memory_space=pl.ANY)
    ring = [pltpu.VMEM((RING_PAGES, page_rows, ATTN_DH), F32), pltpu.SemaphoreType.DMA((RING_PAGES,))]

    bsum, logits = pl.pallas_call(
        functools.partial(_moba_sample_keys_kernel, n_pages=n_pages),
        grid_spec=pltpu.PrefetchScalarGridSpec(
            num_scalar_prefetch=1,
            grid=(b,),
            in_specs=[per_batch(rows, d), cache],
            out_specs=[per_batch(nblk, ATTN_HEADS, ATTN_DH), per_batch(nblk, rows, MOBA_BLOCK)],
            scratch_shapes=ring),
        out_shape=[jax.ShapeDtypeStruct((b, nblk, ATTN_HEADS, ATTN_DH), F32),
                   jax.ShapeDtypeStruct((b, nblk, rows, MOBA_BLOCK), F32)],
        compiler_params=_params("arbitrary"),
        name="moba_sample_keys",
    )(page_table, wqs, ck)

    return pl.pallas_call(
        functools.partial(_moba_sample_values_kernel, n_new=l, n_past_blocks=nblk, n_pages=n_pages),
        grid_spec=pltpu.PrefetchScalarGridSpec(
            num_scalar_prefetch=1,
            grid=(b,),
            in_specs=[per_batch(rows, d), per_batch(rows, d), per_batch(nblk, d), per_batch(l, d), per_batch(l, d),
                      per_batch(nblk, rows, MOBA_BLOCK), cache],
            out_specs=per_batch(l, d),
            scratch_shapes=ring + [pltpu.VMEM((rows, 128), F32), pltpu.VMEM((rows, 128), F32),
                                   pltpu.VMEM((rows, 128), F32), pltpu.VMEM((rows, d), F32)]),
        out_shape=jax.ShapeDtypeStruct((b, l, d), F32),
        compiler_params=_params("arbitrary"),
        name="moba_sample_values",
    )(page_table, wq, wqs, bsum.reshape(b, nblk, d), k_new, v_new, logits, cv)


def kernel(x_prompt, x_sample, state_ret, cache_k, cache_v, page_table, ret_norm_g, ret_w_in, ret_w_out,
           attn_norm_g, attn_w_q, attn_w_out, kv_norm_g, w_kv, mlp_norm_g, mlp_w_up, mlp_w_down, final_norm_g):
    bp, lp, d = x_prompt.shape
    bs, ls, _ = x_sample.shape
    assert ret_w_in.shape[0] == 1 and attn_w_q.shape[0] == 1 and mlp_w_up.shape[0] == 2

    def trunk(x, pos, s0, attend):
        b, l, _ = x.shape
        h = x.reshape(b * l, d)
        act = BF16 if l % RET_ROWS == 0 else F32
        proj = norm_matmul(h, ret_norm_g[0], ret_w_in[0], act)
        o, s = retention(proj.reshape(b, l, -1), pos, s0)
        h = matmul_residual(o.reshape(b * l, RET_V).astype(BF16), ret_w_out[0], h)
        h = mlp(h, mlp_norm_g[0], mlp_w_up, mlp_w_down, 0)
        k, v, q = kvq_proj(h, kv_norm_g, attn_norm_g[0], w_kv, attn_w_q[0])
        o = attend(q.reshape(b, l, d), k.reshape(b, l, d), v.reshape(b, l, d))
        h = matmul_residual(o.reshape(b * l, d).astype(BF16), attn_w_out[0], h)
        y = mlp(h, mlp_norm_g[1], mlp_w_up, mlp_w_down, 1, g_final=final_norm_g)
        kv_shape = (b, l, ATTN_HEADS, ATTN_DH)
        return y.reshape(b, l, d), s[None], k.reshape(kv_shape), v.reshape(kv_shape)

    y_p, s_p, k_p, v_p = trunk(x_prompt, jnp.arange(lp, dtype=jnp.int32), None, moba_prompt)
    y_s, s_s, k_s, v_s = trunk(
        x_sample, PAST_LEN + jnp.arange(ls, dtype=jnp.int32), state_ret[0],
        lambda q, k, v: moba_sample(q, k, v, cache_k, cache_v, page_table))
    return (y_p, y_s, s_p, s_s, k_p, v_p, k_s, v_s)
```

```python
import functools
import math

import jax
import jax.numpy as jnp
from jax import lax
from jax.experimental import pallas as pl
from jax.experimental.pallas import tpu as pltpu

F32 = jnp.float32
BF16 = jnp.bfloat16

D_MODEL = 1024
PAST_LEN = 8192
PAGE_SIZE = 128
RET_HEADS = 4
RET_DK = D_MODEL // RET_HEADS
RET_QK = RET_HEADS * RET_DK
RET_V = 2 * D_MODEL
RET_DV = RET_V // RET_HEADS
RET_CHUNK = 128
ATTN_HEADS = 8
ATTN_DH = D_MODEL // ATTN_HEADS
MOBA_BLOCK = 256
MOBA_TOPK = 3
D_FF = 4 * D_MODEL
NORM_EPS = 1e-5
GN_EPS = 1e-6
NEG = -1e30
LOG2_E = math.log2(math.e)

PAGES_PER_BLOCK = MOBA_BLOCK // PAGE_SIZE
STEP_PAGES = 8
STEP_BLOCKS = STEP_PAGES // PAGES_PER_BLOCK
RING_PAGES = 3 * STEP_PAGES
ROW_CHUNKS = 4
RET_ROWS = 128
VMEM_LIMIT_BYTES = 56 * 1024 * 1024

_NT = (((1,), (1,)), ((), ()))
_TN = (((0,), (0,)), ((), ()))


def _params(*sem):
    return pltpu.CompilerParams(dimension_semantics=sem, vmem_limit_bytes=VMEM_LIMIT_BYTES)


def _rms(x, g, eps=NORM_EPS):
    return x * lax.rsqrt(jnp.mean(x * x, axis=-1, keepdims=True) + eps) * g


def _row_chunks(rows):
    size = rows // ROW_CHUNKS if rows % (ROW_CHUNKS * 16) == 0 else rows
    return [slice(r, r + size) for r in range(0, rows, size)]


def _norm_matmul_kernel(x_ref, g_ref, w_ref, o_ref, wb_ref):
    @pl.when(pl.program_id(1) == 0)
    def _():
        wb_ref[...] = w_ref[...].astype(BF16)

    for rows in _row_chunks(x_ref.shape[0]):
        xn = _rms(x_ref[rows, :], g_ref[...]).astype(BF16)
        o_ref[rows, :] = jnp.dot(xn, wb_ref[...], preferred_element_type=F32).astype(o_ref.dtype)


def norm_matmul(x, g, w, out_dtype, tm=1024, tn=2048):
    m, d = x.shape
    n = w.shape[1]
    tm = min(tm, m)
    return pl.pallas_call(
        _norm_matmul_kernel,
        grid=(n // tn, m // tm),
        in_specs=[pl.BlockSpec((tm, d), lambda j, i: (i, 0)),
                  pl.BlockSpec((1, d), lambda j, i: (0, 0)),
                  pl.BlockSpec((d, tn), lambda j, i: (0, j))],
        out_specs=pl.BlockSpec((tm, tn), lambda j, i: (i, j)),
        out_shape=jax.ShapeDtypeStruct((m, n), out_dtype),
        scratch_shapes=[pltpu.VMEM((d, tn), BF16)],
        compiler_params=_params("parallel", "arbitrary"),
        name="norm_matmul",
    )(x, g.reshape(1, d), w)


def _resident_weight(shape):
    return pl.BlockSpec(shape, lambda i: (0,) * len(shape), pipeline_mode=pl.Buffered(1))


def _matmul_residual_kernel(a_ref, w_ref, r_ref, o_ref, wb_ref):
    @pl.when(pl.program_id(0) == 0)
    def _():
        wb_ref[...] = w_ref[...].astype(BF16)

    o_ref[...] = r_ref[...] + jnp.dot(a_ref[...], wb_ref[...], preferred_element_type=F32)


def matmul_residual(a, w, res, tm=1024):
    m, k = a.shape
    n = w.shape[1]
    tm = min(tm, m)
    return pl.pallas_call(
        _matmul_residual_kernel,
        grid=(m // tm,),
        in_specs=[pl.BlockSpec((tm, k), lambda i: (i, 0)),
                  _resident_weight((k, n)),
                  pl.BlockSpec((tm, n), lambda i: (i, 0))],
        out_specs=pl.BlockSpec((tm, n), lambda i: (i, 0)),
        out_shape=jax.ShapeDtypeStruct((m, n), F32),
        scratch_shapes=[pltpu.VMEM((k, n), BF16)],
        compiler_params=_params("arbitrary"),
        name="matmul_residual",
    )(a, w, res)


def _mlp_kernel(x_ref, g_ref, wu_ref, wd_ref, gf_ref, o_ref, xn_ref, acc_ref, *, final_norm):
    f = pl.program_id(1)
    last = pl.num_programs(1) - 1

    def weights():
        return wu_ref[...].astype(BF16), wd_ref[...].astype(BF16)

    def hidden(xn, wu, wd):
        u = jnp.maximum(jnp.dot(xn, wu, preferred_element_type=F32), 0.0)
        return jnp.dot((u * u).astype(BF16), wd, preferred_element_type=F32)

    @pl.when(f == 0)
    def _():
        wu, wd = weights()
        for rows in _row_chunks(x_ref.shape[0]):
            xn = _rms(x_ref[rows, :], g_ref[...]).astype(BF16)
            xn_ref[rows, :] = xn
            acc_ref[rows, :] = hidden(xn, wu, wd)

    @pl.when(jnp.logical_and(f != 0, f != last))
    def _():
        acc_ref[...] += hidden(xn_ref[...], *weights())

    @pl.when(f == last)
    def _():
        wu, wd = weights()
        for rows in _row_chunks(x_ref.shape[0]):
            h = x_ref[rows, :] + acc_ref[rows, :] + hidden(xn_ref[rows, :], wu, wd)
            o_ref[rows, :] = _rms(h, gf_ref[...]) if final_norm else h


def mlp(x, g, w_up, w_down, layer, g_final=None, tm=1024, tf=1024):
    m, d = x.shape
    ff = w_up.shape[2]
    assert ff // tf >= 2
    tm = min(tm, m)
    final_norm = g_final is not None
    gf = g_final if final_norm else g
    return pl.pallas_call(
        functools.partial(_mlp_kernel, final_norm=final_norm),
        grid=(m // tm, ff // tf),
        in_specs=[pl.BlockSpec((tm, d), lambda i, f: (i, 0)),
                  pl.BlockSpec((1, d), lambda i, f: (0, 0)),
                  pl.BlockSpec((None, d, tf), lambda i, f: (layer, 0, f)),
                  pl.BlockSpec((None, tf, d), lambda i, f: (layer, f, 0)),
                  pl.BlockSpec((1, d), lambda i, f: (0, 0))],
        out_specs=pl.BlockSpec((tm, d), lambda i, f: (i, 0)),
        out_shape=jax.ShapeDtypeStruct((m, d), F32),
        scratch_shapes=[pltpu.VMEM((tm, d), BF16), pltpu.VMEM((tm, d), F32)],
        compiler_params=_params("parallel", "arbitrary"),
        name="mlp",
    )(x, g.reshape(1, d), w_up, w_down, gf.reshape(1, d))


def _kvq_kernel(x_ref, gkv_ref, gq_ref, wkv_ref, wq_ref, k_ref, v_ref, q_ref, wkvb_ref, wqb_ref):
    @pl.when(pl.program_id(0) == 0)
    def _():
        wkvb_ref[...] = wkv_ref[...].astype(BF16)
        wqb_ref[...] = wq_ref[...].astype(BF16)

    d = x_ref.shape[1]
    x = x_ref[...]
    r = x * lax.rsqrt(jnp.mean(x * x, axis=-1, keepdims=True) + NORM_EPS)
    xkv = (r * gkv_ref[...]).astype(BF16)
    xq = (r * gq_ref[...]).astype(BF16)
    k_ref[...] = jnp.dot(xkv, wkvb_ref[:, :d], preferred_element_type=F32)
    v_ref[...] = jnp.dot(xkv, wkvb_ref[:, d:], preferred_element_type=F32)
    q_ref[...] = jnp.dot(xq, wqb_ref[...], preferred_element_type=F32)


def kvq_proj(x, g_kv, g_q, w_kv, w_q, tm=512):
    m, d = x.shape
    tm = min(tm, m)
    row = pl.BlockSpec((tm, d), lambda i: (i, 0))
    gain = pl.BlockSpec((1, d), lambda i: (0, 0))
    out = jax.ShapeDtypeStruct((m, d), F32)
    return pl.pallas_call(
        _kvq_kernel,
        grid=(m // tm,),
        in_specs=[row, gain, gain, _resident_weight((d, 2 * d)), _resident_weight((d, d))],
        out_specs=[row, row, row],
        out_shape=[out, out, out],
        scratch_shapes=[pltpu.VMEM((d, 2 * d), BF16), pltpu.VMEM((d, d), BF16)],
        compiler_params=_params("arbitrary"),
        name="kvq_proj",
    )(x, g_kv.reshape(1, d), g_q.reshape(1, d), w_kv, w_q)


def _ret_log_decay():
    return jnp.log(1.0 - 2.0 ** (-5.0 - jnp.arange(RET_HEADS, dtype=F32)))


def _ret_tables(pos, chunk):
    angle = 1.0 / (10000.0 ** jnp.linspace(0.0, 1.0, RET_DK // 2, dtype=F32))
    angle = jnp.repeat(angle, 2)
    ang = pos.astype(F32)[:, None] * angle[None, :]
    sin, cos = jnp.sin(ang), jnp.cos(ang)
    even = (jnp.arange(RET_DK) % 2 == 0)[None, :]
    sin_next = jnp.where(even, -sin, 0.0)
    sin_prev = jnp.where(even, 0.0, sin)

    lg = _ret_log_decay()
    i = jnp.arange(chunk, dtype=F32)
    diff = i[:, None] - i[None, :]
    d_intra = jnp.exp(jnp.where(diff >= 0, lg[:, None, None] * diff, -jnp.inf))
    q_dec = jnp.exp(lg[:, None] * (i + 1.0))
    k_dec = jnp.exp(lg[:, None] * (chunk - 1.0 - i))
    c_dec = jnp.exp(lg * chunk)
    d_intra = d_intra * RET_DK ** -0.5
    k_dec = k_dec * RET_DK ** -0.5
    pad = RET_ROWS - chunk
    d_intra = jnp.pad(d_intra, ((0, 0), (0, pad), (0, pad)))
    q_dec = jnp.broadcast_to(jnp.pad(q_dec, ((0, 0), (0, pad)))[:, :, None], (RET_HEADS, RET_ROWS, RET_DK))
    k_dec = jnp.broadcast_to(jnp.pad(k_dec, ((0, 0), (0, pad)))[:, :, None], (RET_HEADS, RET_ROWS, RET_DK))
    c_dec = jnp.broadcast_to(c_dec[:, None, None], (RET_HEADS, 1, RET_DV))
    return cos, sin_next, sin_prev, d_intra, q_dec, k_dec, c_dec


def _theta_shift(x, cos, sin_next, sin_prev):
    dk = x.shape[-1]
    return x * cos + pltpu.roll(x, dk - 1, 1) * sin_next + pltpu.roll(x, 1, 1) * sin_prev


def _pad_rows(x, rows):
    if x.shape[0] == rows:
        return x
    return jnp.concatenate([x, jnp.zeros((rows - x.shape[0], x.shape[1]), x.dtype)], axis=0)


def _retention_kernel(q_ref, k_ref, v_ref, gate_ref, cos_ref, sn_ref, sp_ref, dintra_ref, qdec_ref, kdec_ref,
                      cdec_ref, s0_ref, o_ref, sout_ref, *stream, chunk, n_chunks, n_batch, has_state):
    seeded = n_chunks > 1
    streamed = bool(stream)
    if seeded:
        @pl.when(pl.program_id(1) == 0)
        def _():
            sout_ref[0] = s0_ref[0] if has_state else jnp.zeros(sout_ref.shape[1:], F32)
    if streamed:
        sin_buf, sout_buf, sem_in, sem_out = stream
        b = pl.program_id(0)
        slot = b % 2

        def fetch(bb, sl, h):
            return pltpu.make_async_copy(s0_ref.at[bb, h], sin_buf.at[sl, h], sem_in.at[sl, h])

        def flush(bb, sl, h):
            return pltpu.make_async_copy(sout_buf.at[sl, h], sout_ref.at[bb, h], sem_out.at[sl, h])

        @pl.when(b == 0)
        def _():
            for h in range(RET_HEADS):
                fetch(0, 0, h).start()

        @pl.when(b + 1 < n_batch)
        def _():
            for h in range(RET_HEADS):
                fetch(b + 1, 1 - slot, h).start()

        @pl.when(b >= 2)
        def _():
            for h in range(RET_HEADS):
                flush(b - 2, slot, h).wait()

    cos, sn, sp = cos_ref[...], sn_ref[...], sp_ref[...]

    def scores(h):
        qk_cols = slice(h * RET_DK, (h + 1) * RET_DK)
        q = _theta_shift(q_ref[0, :, qk_cols].astype(F32), cos, sn, sp)
        k = _theta_shift(k_ref[0, :, qk_cols].astype(F32), cos, sn, sp)
        q = _pad_rows(q, RET_ROWS)
        k = _pad_rows(k, RET_ROWS)
        a = lax.dot_general(q.astype(BF16), k.astype(BF16), _NT, preferred_element_type=F32) * dintra_ref[h]
        return q, k, a

    def finish(h, q, k, a):
        v_cols = slice(h * RET_DV, (h + 1) * RET_DV)
        v = _pad_rows(v_ref[0, :, v_cols], RET_ROWS).astype(BF16)
        if streamed:
            fetch(b, slot, h).wait()
            s = sin_buf[slot, h]
        elif seeded:
            s = sout_ref[0, h]
        else:
            s = s0_ref[0, h] if has_state else jnp.zeros((RET_DK, RET_DV), F32)
        o = (jnp.dot(a.astype(BF16), v, preferred_element_type=F32)
             + jnp.dot((q * qdec_ref[h]).astype(BF16), s.astype(BF16), preferred_element_type=F32))
        s_new = s * cdec_ref[h] + lax.dot_general((k * kdec_ref[h]).astype(BF16), v, _TN,
                                                  preferred_element_type=F32)
        if streamed:
            sout_buf[slot, h] = s_new
            flush(b, slot, h).start()
        else:
            sout_ref[0, h] = s_new
        o = o[:chunk]
        o = o * lax.rsqrt(jnp.mean(o * o, axis=-1, keepdims=True) + GN_EPS)
        gate = gate_ref[0, :, v_cols].astype(F32)
        o_ref[0, :, v_cols] = (o * (gate * jax.nn.sigmoid(gate))).astype(o_ref.dtype)

    pending = scores(0)
    for h in range(RET_HEADS):
        following = scores(h + 1) if h + 1 < RET_HEADS else None
        finish(h, *pending)
        pending = following

    if streamed:
        @pl.when(b == n_batch - 1)
        def _():
            for h in range(RET_HEADS):
                if n_batch > 1:
                    flush(b - 1, 1 - slot, h).wait()
                flush(b, slot, h).wait()


def retention(proj, pos, s0):
    b, l, _ = proj.shape
    chunk = math.gcd(l, RET_CHUNK)
    n = l // chunk
    has_state = s0 is not None
    if not has_state:
        s0 = jnp.zeros((1, RET_HEADS, RET_DK, RET_DV), F32)
    tables = _ret_tables(pos, chunk)
    k_col, v_col, gate_col = 1, 2 * RET_QK // RET_V, 2 * RET_QK // RET_V + 1
    pos_tab = pl.BlockSpec((chunk, RET_DK), lambda bi, c: (c, 0))
    head_tab = lambda rows, cols: pl.BlockSpec((RET_HEADS, rows, cols), lambda bi, c: (0, 0, 0))
    state = lambda index_map: pl.BlockSpec((1, RET_HEADS, RET_DK, RET_DV), index_map)
    streamed = has_state and n == 1
    if streamed:
        state_in = state_out = pl.BlockSpec(memory_space=pl.ANY)
        slots = (2, RET_HEADS, RET_DK, RET_DV)
        scratch = [pltpu.VMEM(slots, F32), pltpu.VMEM(slots, F32),
                   pltpu.SemaphoreType.DMA(slots[:2]), pltpu.SemaphoreType.DMA(slots[:2])]
        semantics = ("arbitrary", "arbitrary")
    else:
        state_in = state((lambda bi, c: (bi, 0, 0, 0)) if has_state else (lambda bi, c: (0, 0, 0, 0)))
        state_out = state(lambda bi, c: (bi, 0, 0, 0))
        scratch = []
        semantics = ("parallel", "arbitrary")
    return pl.pallas_call(
        functools.partial(_retention_kernel, chunk=chunk, n_chunks=n, n_batch=b, has_state=has_state),
        grid=(b, n),
        in_specs=[pl.BlockSpec((1, chunk, RET_QK), lambda bi, c: (bi, c, 0)),
                  pl.BlockSpec((1, chunk, RET_QK), lambda bi, c: (bi, c, k_col)),
                  pl.BlockSpec((1, chunk, RET_V), lambda bi, c: (bi, c, v_col)),
                  pl.BlockSpec((1, chunk, RET_V), lambda bi, c: (bi, c, gate_col)),
                  pos_tab, pos_tab, pos_tab,
                  head_tab(RET_ROWS, RET_ROWS), head_tab(RET_ROWS, RET_DK), head_tab(RET_ROWS, RET_DK),
                  head_tab(1, RET_DV), state_in],
        out_specs=[pl.BlockSpec((1, chunk, RET_V), lambda bi, c: (bi, c, 0)), state_out],
        out_shape=[jax.ShapeDtypeStruct((b, l, RET_V), proj.dtype),
                   jax.ShapeDtypeStruct((b, RET_HEADS, RET_DK, RET_DV), F32)],
        scratch_shapes=scratch,
        compiler_params=_params(*semantics),
        name="retention",
    )(proj, proj, proj, proj, *tables, s0)


def _moba_prompt_kernel(q_ref, k_ref, v_ref, o_ref, *, nb):
    k = k_ref[0]
    s_len = k.shape[0]
    means = jnp.sum(k.reshape(nb, MOBA_BLOCK, ATTN_DH), axis=1) * (1.0 / MOBA_BLOCK)
    means = _pad_rows(means, -(-nb // 8) * 8)
    blk = lax.broadcasted_iota(jnp.int32, (means.shape[0], MOBA_BLOCK), 0)
    key_lane = lax.broadcasted_iota(jnp.int32, (s_len, 128), 1)
    key_block = lax.broadcasted_iota(jnp.int32, (s_len, 128), 0) // MOBA_BLOCK
    kb = jnp.concatenate([k.astype(BF16), jnp.where(key_block == key_lane, 1.0, 0.0).astype(BF16)], axis=1)
    vb = jnp.concatenate([v_ref[0].astype(BF16), jnp.where(key_lane == 0, 1.0, 0.0).astype(BF16)], axis=1)
    row = lax.broadcasted_iota(jnp.int32, (MOBA_BLOCK, MOBA_BLOCK), 0)
    col = lax.broadcasted_iota(jnp.int32, (MOBA_BLOCK, MOBA_BLOCK), 1)
    causal = col <= row
    eye = jnp.where(col == row, 1.0, 0.0).astype(BF16)

    def masked_logits(i):
        q = q_ref[0, i * MOBA_BLOCK:(i + 1) * MOBA_BLOCK, :]
        bias = jnp.zeros((MOBA_BLOCK, 128), BF16)
        if i > 0:
            s = lax.dot_general(means, q, _NT, preferred_element_type=F32, precision=lax.Precision.HIGHEST)
            elig = blk < i
            sm = jnp.where(elig, s, NEG)
            rank = jnp.zeros(sm.shape, F32)
            for m in range(i):
                cm = sm[m:m + 1, :]
                beats = jnp.where(cm > sm, 1.0, jnp.where(cm == sm, jnp.where(blk > m, 1.0, 0.0), 0.0))
                rank = rank + beats
            keep = jnp.where(rank < MOBA_TOPK, jnp.where(sm > 0.5 * NEG, 0.0, NEG), NEG)
            bias_t = _pad_rows(jnp.where(elig, keep, 0.0), 128).astype(BF16)
            bias = lax.dot_general(eye, bias_t, _NT, preferred_element_type=F32).astype(BF16)
        qa = jnp.concatenate([(q * (ATTN_DH ** -0.5 * LOG2_E)).astype(BF16), bias], axis=1)
        kv_rows = (i + 1) * MOBA_BLOCK
        logits = lax.dot_general(qa, kb[:kv_rows], _NT, preferred_element_type=F32)
        own = jnp.where(causal, logits[:, i * MOBA_BLOCK:], NEG)
        return own if i == 0 else jnp.concatenate([logits[:, :i * MOBA_BLOCK], own], axis=1)

    def attend(i, logits):
        p = jnp.exp2(logits - jnp.max(logits, axis=-1, keepdims=True))
        acc = jnp.dot(p.astype(BF16), vb[:(i + 1) * MOBA_BLOCK], preferred_element_type=F32)
        o_ref[0, i * MOBA_BLOCK:(i + 1) * MOBA_BLOCK, :] = (
            acc[:, :ATTN_DH] / acc[:, ATTN_DH:ATTN_DH + 1]).astype(o_ref.dtype)

    order = list(range(nb - 1, -1, -1))
    logits = masked_logits(order[0])
    for n, i in enumerate(order):
        following = masked_logits(order[n + 1]) if n + 1 < nb else None
        attend(i, logits)
        logits = following


def moba_prompt(q, k, v):
    b, s, _ = q.shape
    assert s % MOBA_BLOCK == 0
    nb = s // MOBA_BLOCK
    spec = pl.BlockSpec((1, s, ATTN_DH), lambda bi, h: (bi, 0, h))
    return pl.pallas_call(
        functools.partial(_moba_prompt_kernel, nb=nb),
        grid=(b, ATTN_HEADS),
        in_specs=[spec, spec, spec],
        out_specs=spec,
        out_shape=jax.ShapeDtypeStruct((b, s, ATTN_HEADS * ATTN_DH), BF16),
        compiler_params=_params("parallel", "parallel"),
        name="moba_prompt",
    )(q, k, v)


def _block_diag_queries(q):
    b, l, h, dh = q.shape
    eye = jnp.eye(h, dtype=q.dtype)
    w = q.transpose(0, 2, 1, 3)[:, :, :, None, :] * eye[None, :, None, :, None]
    return w.reshape(b, h * l, h * dh)


def _page_rows(page_ref):
    heads = [page_ref[pl.ds(h, PAGE_SIZE, stride=ATTN_HEADS), :] for h in range(ATTN_HEADS)]
    return jnp.concatenate(heads, axis=1).astype(BF16)


class _PageStream:
    def __init__(self, pt_ref, cache_refs, buf_ref, sem_ref, n_pages):
        assert n_pages % STEP_PAGES == 0
        self.pt_ref, self.cache_refs, self.buf_ref, self.sem_ref = pt_ref, cache_refs, buf_ref, sem_ref
        self.n_pages = n_pages
        self.span = len(cache_refs) * n_pages
        self.total = pl.num_programs(0) * self.span
        self.base = pl.program_id(0) * self.span

    def _copy(self, cache_ref, page, g):
        slot = g % RING_PAGES
        return pltpu.make_async_copy(cache_ref.at[page], self.buf_ref.at[slot], self.sem_ref.at[slot])

    def _start_group(self, g0):
        seq, within = g0 // self.span, g0 % self.span
        which, first = within // self.n_pages, within % self.n_pages
        for c, cache_ref in enumerate(self.cache_refs):
            @pl.when(which == c)
            def _(cache_ref=cache_ref):
                for j in range(STEP_PAGES):
                    self._copy(cache_ref, self.pt_ref[seq, first + j], g0 + j).start()

    def prime(self):
        @pl.when(pl.program_id(0) == 0)
        def _():
            for g0 in range(0, RING_PAGES - STEP_PAGES, STEP_PAGES):
                self._start_group(jnp.int32(g0))

    def group(self, it):
        g0 = self.base + it * STEP_PAGES
        ahead = g0 + (RING_PAGES - STEP_PAGES)

        @pl.when(ahead < self.total)
        def _():
            self._start_group(ahead)

        pages = []
        for j in range(STEP_PAGES):
            self._copy(self.cache_refs[0], 0, g0 + j).wait()
            pages.append(self.buf_ref.at[(g0 + j) % RING_PAGES])
        return pages


def _moba_sample_kernel(pt_ref, wq_ref, wqs_ref, kn_ref, vn_ref, ck_ref, cv_ref, o_ref,
                        buf_ref, sem_ref, lg_ref, bsum_ref, sel_ref, m_ref, l_ref, acc_ref,
                        *, n_new, n_past_blocks, n_pages):
    stream = _PageStream(pt_ref, (ck_ref, cv_ref), buf_ref, sem_ref, n_pages)
    stream.prime()
    groups = n_pages // STEP_PAGES
    rows = wq_ref.shape[1]
    lane = lax.broadcasted_iota(jnp.int32, (rows, 128), 1)
    w = wqs_ref[0]

    def keys_body(it, carry):
        pages = stream.group(it)
        sums = [jnp.sum(p[...].reshape(PAGE_SIZE, ATTN_HEADS, ATTN_DH), axis=0) for p in pages]
        keys = jnp.concatenate([_page_rows(p) for p in pages], axis=0)
        logits = lax.dot_general(w, keys, _NT, preferred_element_type=F32)
        for j in range(STEP_BLOCKS):
            blk = it * STEP_BLOCKS + j
            bsum_ref[pl.ds(pl.multiple_of(blk * ATTN_HEADS, ATTN_HEADS), ATTN_HEADS), :] = functools.reduce(
                lambda x, y: x + y, sums[j * PAGES_PER_BLOCK:(j + 1) * PAGES_PER_BLOCK])
            lg_ref[blk] = logits[:, j * MOBA_BLOCK:(j + 1) * MOBA_BLOCK]
        return carry

    lax.fori_loop(0, groups, keys_body, 0)

    kn = kn_ref[0]
    past_sums = jnp.concatenate(
        [bsum_ref[pl.ds(h, n_past_blocks, stride=ATTN_HEADS), :] for h in range(ATTN_HEADS)], axis=1)
    first_row = lax.broadcasted_iota(jnp.int32, (8, kn.shape[1]), 0) == 0
    own_sum = jnp.where(first_row, jnp.sum(kn, axis=0, keepdims=True), 0.0)
    means = _pad_rows(jnp.concatenate([past_sums, own_sum], axis=0) * (1.0 / MOBA_BLOCK), 128)
    s = lax.dot_general(wq_ref[0], means, _NT, preferred_element_type=F32, precision=lax.Precision.HIGHEST)
    work = jnp.where(lane < n_past_blocks, s, NEG)
    sel = jnp.zeros(work.shape, F32)
    for _ in range(MOBA_TOPK):
        mx = jnp.max(work, axis=-1, keepdims=True)
        first = jnp.min(jnp.where(work == mx, lane, 128), axis=-1, keepdims=True)
        pick = lane == first
        sel = jnp.where(pick, jnp.where(mx > 0.5 * NEG, 1.0, 0.0), sel)
        work = jnp.where(pick, -jnp.inf, work)
    sel_ref[...] = sel
    t = lax.broadcasted_iota(jnp.int32, (rows, 128), 0) % n_new
    lo = lax.dot_general(wqs_ref[0], _pad_rows(kn, 128).astype(BF16), _NT, preferred_element_type=F32)
    lo = jnp.where(lane <= t, lo, NEG)
    m0 = jnp.max(lo, axis=-1, keepdims=True)
    p0 = jnp.exp(lo - m0)
    m_ref[...] = jnp.broadcast_to(m0, m_ref.shape)
    l_ref[...] = jnp.broadcast_to(jnp.sum(p0, axis=-1, keepdims=True), l_ref.shape)
    acc_ref[...] = jnp.dot(p0.astype(BF16), _pad_rows(vn_ref[0], 128).astype(BF16), preferred_element_type=F32)

    def values_body(it, carry):
        pages = stream.group(groups + it)
        sel = sel_ref[...]
        lgs = []
        for j in range(STEP_BLOCKS):
            blk = it * STEP_BLOCKS + j
            picked = jnp.max(jnp.where(lane == blk, sel, 0.0), axis=-1, keepdims=True) > 0.5
            lgs.append(jnp.where(picked, lg_ref[blk], NEG))
        m_old = m_ref[:, :1]
        m_new = functools.reduce(jnp.maximum, [m_old] + [jnp.max(lg, axis=-1, keepdims=True) for lg in lgs])
        alpha = jnp.exp(m_old - m_new)
        ps = [jnp.exp(lg - m_new) for lg in lgs]
        m_ref[...] = jnp.broadcast_to(m_new, m_ref.shape)
        l_ref[...] = alpha * l_ref[...] + functools.reduce(
            lambda x, y: x + y, [jnp.sum(p, axis=-1, keepdims=True) for p in ps])
        values = jnp.concatenate([_page_rows(p) for p in pages], axis=0)
        acc_ref[...] = alpha * acc_ref[...] + jnp.dot(
            jnp.concatenate(ps, axis=1).astype(BF16), values, preferred_element_type=F32)
        return carry

    lax.fori_loop(0, groups, values_body, 0)

    out = acc_ref[...] / l_ref[:, :1]
    for h in range(ATTN_HEADS):
        o_ref[0, :, h * ATTN_DH:(h + 1) * ATTN_DH] = (
            out[h * n_new:(h + 1) * n_new, h * ATTN_DH:(h + 1) * ATTN_DH].astype(o_ref.dtype))


def moba_sample(q, k_new, v_new, cache_k, cache_v, page_table):
    b, l, d = q.shape
    n_pages = PAST_LEN // PAGE_SIZE
    assert n_pages % STEP_PAGES == 0 and (PAST_LEN + l - 1) // MOBA_BLOCK == PAST_LEN // MOBA_BLOCK
    nblk = n_pages // PAGES_PER_BLOCK
    rows = ATTN_HEADS * l
    page_rows = PAGE_SIZE * ATTN_HEADS
    ck = cache_k.reshape(-1, page_rows, ATTN_DH)
    cv = cache_v.reshape(-1, page_rows, ATTN_DH)
    wq = _block_diag_queries(q.reshape(b, l, ATTN_HEADS, ATTN_DH))
    wqs = (wq * ATTN_DH ** -0.5).astype(BF16)

    per_batch = lambda *shape: pl.BlockSpec((1,) + shape, lambda bi, pt: (bi,) + (0,) * len(shape))
    cache = pl.BlockSpec(memory_space=pl.ANY)
    return pl.pallas_call(
        functools.partial(_moba_sample_kernel, n_new=l, n_past_blocks=nblk, n_pages=n_pages),
        grid_spec=pltpu.PrefetchScalarGridSpec(
            num_scalar_prefetch=1,
            grid=(b,),
            in_specs=[per_batch(rows, d), per_batch(rows, d), per_batch(l, d), per_batch(l, d), cache, cache],
            out_specs=per_batch(l, d),
            scratch_shapes=[pltpu.VMEM((RING_PAGES, page_rows, ATTN_DH), F32),
                            pltpu.SemaphoreType.DMA((RING_PAGES,)),
                            pltpu.VMEM((nblk, rows, MOBA_BLOCK), F32),
                            pltpu.VMEM((nblk * ATTN_HEADS, ATTN_DH), F32),
                            pltpu.VMEM((rows, 128), F32),
                            pltpu.VMEM((rows, 128), F32),
                            pltpu.VMEM((rows, 128), F32),
                            pltpu.VMEM((rows, d), F32)]),
        out_shape=jax.ShapeDtypeStruct((b, l, d), F32),
        compiler_params=_params("arbitrary"),
        name="moba_sample",
    )(page_table, wq, wqs, k_new, v_new, ck, cv)


def kernel(x_prompt, x_sample, state_ret, cache_k, cache_v, page_table, ret_norm_g, ret_w_in, ret_w_out,
           attn_norm_g, attn_w_q, attn_w_out, kv_norm_g, w_kv, mlp_norm_g, mlp_w_up, mlp_w_down, final_norm_g):
    bp, lp, d = x_prompt.shape
    bs, ls, _ = x_sample.shape
    assert ret_w_in.shape[0] == 1 and attn_w_q.shape[0] == 1 and mlp_w_up.shape[0] == 2

    def trunk(x, pos, s0, attend):
        b, l, _ = x.shape
        h = x.reshape(b * l, d)
        act = BF16 if l % RET_ROWS == 0 else F32
        proj = norm_matmul(h, ret_norm_g[0], ret_w_in[0], act)
        o, s = retention(proj.reshape(b, l, -1), pos, s0)
        h = matmul_residual(o.reshape(b * l, RET_V).astype(BF16), ret_w_out[0], h)
        h = mlp(h, mlp_norm_g[0], mlp_w_up, mlp_w_down, 0)
        k, v, q = kvq_proj(h, kv_norm_g, attn_norm_g[0], w_kv, attn_w_q[0])
        o = attend(q.reshape(b, l, d), k.reshape(b, l, d), v.reshape(b, l, d))
        h = matmul_residual(o.reshape(b * l, d).astype(BF16), attn_w_out[0], h)
        y = mlp(h, mlp_norm_g[1], mlp_w_up, mlp_w_down, 1, g_final=final_norm_g)
        kv_shape = (b, l, ATTN_HEADS, ATTN_DH)
        return y.reshape(b, l, d), s[None], k.reshape(kv_shape), v.reshape(kv_shape)

    y_p, s_p, k_p, v_p = trunk(x_prompt, jnp.arange(lp, dtype=jnp.int32), None, moba_prompt)
    y_s, s_s, k_s, v_s = trunk(
        x_sample, PAST_LEN + jnp.arange(ls, dtype=jnp.int32), state_ret[0],
        lambda q, k, v: moba_sample(q, k, v, cache_k, cache_v, page_table))
    return (y_p, y_s, s_p, s_s, k_p, v_p, k_s, v_s)
```

```python
import functools
import math

import jax
import jax.numpy as jnp
from jax import lax
from jax.experimental import pallas as pl
from jax.experimental.pallas import tpu as pltpu

F32 = jnp.float32
BF16 = jnp.bfloat16

D_MODEL = 1024
PAST_LEN = 8192
PAGE_SIZE = 128
RET_HEADS = 4
RET_DK = D_MODEL // RET_HEADS
RET_QK = RET_HEADS * RET_DK
RET_V = 2 * D_MODEL
RET_DV = RET_V // RET_HEADS
RET_CHUNK = 128
ATTN_HEADS = 8
ATTN_DH = D_MODEL // ATTN_HEADS
MOBA_BLOCK = 256
MOBA_TOPK = 3
D_FF = 4 * D_MODEL
NORM_EPS = 1e-5
GN_EPS = 1e-6
NEG = -1e30
LOG2_E = math.log2(math.e)

PAGES_PER_BLOCK = MOBA_BLOCK // PAGE_SIZE
STEP_PAGES = 8
STEP_BLOCKS = STEP_PAGES // PAGES_PER_BLOCK
RING_PAGES = 3 * STEP_PAGES
PROMPT_TILE_BLOCKS = 2
ROW_CHUNKS = 4
RET_ROWS = 128
VMEM_LIMIT_BYTES = 56 * 1024 * 1024

_NT = (((1,), (1,)), ((), ()))
_TN = (((0,), (0,)), ((), ()))


def _params(*sem):
    return pltpu.CompilerParams(dimension_semantics=sem, vmem_limit_bytes=VMEM_LIMIT_BYTES)


def _rms(x, g, eps=NORM_EPS):
    return x * lax.rsqrt(jnp.mean(x * x, axis=-1, keepdims=True) + eps) * g


def _row_chunks(rows, chunks=ROW_CHUNKS):
    size = rows // chunks if rows % (chunks * 16) == 0 else rows
    return [slice(r, r + size) for r in range(0, rows, size)]


def _norm_matmul_kernel(x_ref, g_ref, w_ref, o_ref, wb_ref):
    @pl.when(pl.program_id(1) == 0)
    def _():
        wb_ref[...] = w_ref[...].astype(BF16)

    for rows in _row_chunks(x_ref.shape[0]):
        xn = _rms(x_ref[rows, :], g_ref[...]).astype(BF16)
        o_ref[rows, :] = jnp.dot(xn, wb_ref[...], preferred_element_type=F32).astype(o_ref.dtype)


def norm_matmul(x, g, w, out_dtype, tm=1024, tn=2048):
    m, d = x.shape
    n = w.shape[1]
    tm = min(tm, m)
    return pl.pallas_call(
        _norm_matmul_kernel,
        grid=(n // tn, m // tm),
        in_specs=[pl.BlockSpec((tm, d), lambda j, i: (i, 0)),
                  pl.BlockSpec((1, d), lambda j, i: (0, 0)),
                  pl.BlockSpec((d, tn), lambda j, i: (0, j))],
        out_specs=pl.BlockSpec((tm, tn), lambda j, i: (i, j)),
        out_shape=jax.ShapeDtypeStruct((m, n), out_dtype),
        scratch_shapes=[pltpu.VMEM((d, tn), BF16)],
        compiler_params=_params("parallel", "arbitrary"),
        name="norm_matmul",
    )(x, g.reshape(1, d), w)


def _resident_weight(shape):
    return pl.BlockSpec(shape, lambda i: (0,) * len(shape), pipeline_mode=pl.Buffered(1))


def _matmul_residual_kernel(a_ref, w_ref, r_ref, o_ref, wb_ref):
    @pl.when(pl.program_id(0) == 0)
    def _():
        wb_ref[...] = w_ref[...].astype(BF16)

    o_ref[...] = r_ref[...] + jnp.dot(a_ref[...], wb_ref[...], preferred_element_type=F32)


def matmul_residual(a, w, res, tm=1024):
    m, k = a.shape
    n = w.shape[1]
    tm = min(tm, m)
    return pl.pallas_call(
        _matmul_residual_kernel,
        grid=(m // tm,),
        in_specs=[pl.BlockSpec((tm, k), lambda i: (i, 0)),
                  _resident_weight((k, n)),
                  pl.BlockSpec((tm, n), lambda i: (i, 0))],
        out_specs=pl.BlockSpec((tm, n), lambda i: (i, 0)),
        out_shape=jax.ShapeDtypeStruct((m, n), F32),
        scratch_shapes=[pltpu.VMEM((k, n), BF16)],
        compiler_params=_params("arbitrary"),
        name="matmul_residual",
    )(a, w, res)


def _mlp_kernel(x_ref, g_ref, wu_ref, wd_ref, gf_ref, o_ref, xn_ref, acc_ref, *, final_norm):
    f = pl.program_id(1)
    last = pl.num_programs(1) - 1

    def weights():
        return wu_ref[...].astype(BF16), wd_ref[...].astype(BF16)

    def hidden(xn, wu, wd):
        u = jnp.maximum(jnp.dot(xn, wu, preferred_element_type=F32), 0.0)
        return jnp.dot((u * u).astype(BF16), wd, preferred_element_type=F32)

    @pl.when(f == 0)
    def _():
        wu, wd = weights()
        for rows in _row_chunks(x_ref.shape[0]):
            xn = _rms(x_ref[rows, :], g_ref[...]).astype(BF16)
            xn_ref[rows, :] = xn
            acc_ref[rows, :] = hidden(xn, wu, wd)

    @pl.when(jnp.logical_and(f != 0, f != last))
    def _():
        acc_ref[...] += hidden(xn_ref[...], *weights())

    @pl.when(f == last)
    def _():
        wu, wd = weights()
        for rows in _row_chunks(x_ref.shape[0]):
            h = x_ref[rows, :] + acc_ref[rows, :] + hidden(xn_ref[rows, :], wu, wd)
            o_ref[rows, :] = _rms(h, gf_ref[...]) if final_norm else h


def mlp(x, g, w_up, w_down, layer, g_final=None, tm=1024, tf=1024):
    m, d = x.shape
    ff = w_up.shape[2]
    assert ff // tf >= 2
    tm = min(tm, m)
    final_norm = g_final is not None
    gf = g_final if final_norm else g
    return pl.pallas_call(
        functools.partial(_mlp_kernel, final_norm=final_norm),
        grid=(m // tm, ff // tf),
        in_specs=[pl.BlockSpec((tm, d), lambda i, f: (i, 0)),
                  pl.BlockSpec((1, d), lambda i, f: (0, 0)),
                  pl.BlockSpec((None, d, tf), lambda i, f: (layer, 0, f)),
                  pl.BlockSpec((None, tf, d), lambda i, f: (layer, f, 0)),
                  pl.BlockSpec((1, d), lambda i, f: (0, 0))],
        out_specs=pl.BlockSpec((tm, d), lambda i, f: (i, 0)),
        out_shape=jax.ShapeDtypeStruct((m, d), F32),
        scratch_shapes=[pltpu.VMEM((tm, d), BF16), pltpu.VMEM((tm, d), F32)],
        compiler_params=_params("parallel", "arbitrary"),
        name="mlp",
    )(x, g.reshape(1, d), w_up, w_down, gf.reshape(1, d))


def _kvq_kernel(x_ref, gkv_ref, gq_ref, wkv_ref, wq_ref, k_ref, v_ref, q_ref, wkvb_ref, wqb_ref):
    @pl.when(pl.program_id(0) == 0)
    def _():
        wkvb_ref[...] = wkv_ref[...].astype(BF16)
        wqb_ref[...] = wq_ref[...].astype(BF16)

    d = x_ref.shape[1]
    for rows in _row_chunks(x_ref.shape[0], 2):
        x = x_ref[rows, :]
        r = x * lax.rsqrt(jnp.mean(x * x, axis=-1, keepdims=True) + NORM_EPS)
        xkv = (r * gkv_ref[...]).astype(BF16)
        xq = (r * gq_ref[...]).astype(BF16)
        k_ref[rows, :] = jnp.dot(xkv, wkvb_ref[:, :d], preferred_element_type=F32)
        v_ref[rows, :] = jnp.dot(xkv, wkvb_ref[:, d:], preferred_element_type=F32)
        q_ref[rows, :] = jnp.dot(xq, wqb_ref[...], preferred_element_type=F32)


def kvq_proj(x, g_kv, g_q, w_kv, w_q, tm=512):
    m, d = x.shape
    tm = min(tm, m)
    row = pl.BlockSpec((tm, d), lambda i: (i, 0))
    gain = pl.BlockSpec((1, d), lambda i: (0, 0))
    out = jax.ShapeDtypeStruct((m, d), F32)
    return pl.pallas_call(
        _kvq_kernel,
        grid=(m // tm,),
        in_specs=[row, gain, gain, _resident_weight((d, 2 * d)), _resident_weight((d, d))],
        out_specs=[row, row, row],
        out_shape=[out, out, out],
        scratch_shapes=[pltpu.VMEM((d, 2 * d), BF16), pltpu.VMEM((d, d), BF16)],
        compiler_params=_params("arbitrary"),
        name="kvq_proj",
    )(x, g_kv.reshape(1, d), g_q.reshape(1, d), w_kv, w_q)


def _ret_log_decay():
    return jnp.log(1.0 - 2.0 ** (-5.0 - jnp.arange(RET_HEADS, dtype=F32)))


def _ret_tables(pos, chunk):
    angle = 1.0 / (10000.0 ** jnp.linspace(0.0, 1.0, RET_DK // 2, dtype=F32))
    angle = jnp.repeat(angle, 2)
    ang = pos.astype(F32)[:, None] * angle[None, :]
    sin, cos = jnp.sin(ang), jnp.cos(ang)
    even = (jnp.arange(RET_DK) % 2 == 0)[None, :]
    sin_next = jnp.where(even, -sin, 0.0)
    sin_prev = jnp.where(even, 0.0, sin)

    lg = _ret_log_decay()
    i = jnp.arange(chunk, dtype=F32)
    diff = i[:, None] - i[None, :]
    d_intra = jnp.exp(jnp.where(diff >= 0, lg[:, None, None] * diff, -jnp.inf))
    q_dec = jnp.exp(lg[:, None] * (i + 1.0))
    k_dec = jnp.exp(lg[:, None] * (chunk - 1.0 - i))
    c_dec = jnp.exp(lg * chunk)
    d_intra = d_intra * RET_DK ** -0.5
    k_dec = k_dec * RET_DK ** -0.5
    pad = RET_ROWS - chunk
    d_intra = jnp.pad(d_intra, ((0, 0), (0, pad), (0, pad)))
    q_dec = jnp.broadcast_to(jnp.pad(q_dec, ((0, 0), (0, pad)))[:, :, None], (RET_HEADS, RET_ROWS, RET_DK))
    k_dec = jnp.broadcast_to(jnp.pad(k_dec, ((0, 0), (0, pad)))[:, :, None], (RET_HEADS, RET_ROWS, RET_DK))
    c_dec = jnp.broadcast_to(c_dec[:, None, None], (RET_HEADS, 1, RET_DV))
    return cos, sin_next, sin_prev, d_intra, q_dec, k_dec, c_dec


def _theta_shift(x, cos, sin_next, sin_prev):
    dk = x.shape[-1]
    return x * cos + pltpu.roll(x, dk - 1, 1) * sin_next + pltpu.roll(x, 1, 1) * sin_prev


def _pad_rows(x, rows):
    if x.shape[0] == rows:
        return x
    return jnp.concatenate([x, jnp.zeros((rows - x.shape[0], x.shape[1]), x.dtype)], axis=0)


def _retention_kernel(q_ref, k_ref, v_ref, gate_ref, cos_ref, sn_ref, sp_ref, dintra_ref, qdec_ref, kdec_ref,
                      cdec_ref, s0_ref, o_ref, sout_ref, *stream, chunk, n_chunks, n_batch, has_state):
    seeded = n_chunks > 1
    streamed = bool(stream)
    if seeded:
        @pl.when(pl.program_id(1) == 0)
        def _():
            sout_ref[0] = s0_ref[0] if has_state else jnp.zeros(sout_ref.shape[1:], F32)
    if streamed:
        sin_buf, sout_buf, sem_in, sem_out = stream
        b = pl.program_id(0)
        slot = b % 2

        def fetch(bb, sl, h):
            return pltpu.make_async_copy(s0_ref.at[bb, h], sin_buf.at[sl, h], sem_in.at[sl, h])

        def flush(bb, sl, h):
            return pltpu.make_async_copy(sout_buf.at[sl, h], sout_ref.at[bb, h], sem_out.at[sl, h])

        @pl.when(b == 0)
        def _():
            for h in range(RET_HEADS):
                fetch(0, 0, h).start()

        @pl.when(b + 1 < n_batch)
        def _():
            for h in range(RET_HEADS):
                fetch(b + 1, 1 - slot, h).start()

        @pl.when(b >= 2)
        def _():
            for h in range(RET_HEADS):
                flush(b - 2, slot, h).wait()

    cos, sn, sp = cos_ref[...], sn_ref[...], sp_ref[...]

    def scores(h):
        qk_cols = slice(h * RET_DK, (h + 1) * RET_DK)
        q = _theta_shift(q_ref[0, :, qk_cols].astype(F32), cos, sn, sp)
        k = _theta_shift(k_ref[0, :, qk_cols].astype(F32), cos, sn, sp)
        q = _pad_rows(q, RET_ROWS)
        k = _pad_rows(k, RET_ROWS)
        a = lax.dot_general(q.astype(BF16), k.astype(BF16), _NT, preferred_element_type=F32) * dintra_ref[h]
        return q, k, a

    def finish(h, q, k, a):
        v_cols = slice(h * RET_DV, (h + 1) * RET_DV)
        v = _pad_rows(v_ref[0, :, v_cols], RET_ROWS).astype(BF16)
        if streamed:
            fetch(b, slot, h).wait()
            s = sin_buf[slot, h]
        elif seeded:
            s = sout_ref[0, h]
        else:
            s = s0_ref[0, h] if has_state else jnp.zeros((RET_DK, RET_DV), F32)
        o = (jnp.dot(a.astype(BF16), v, preferred_element_type=F32)
             + jnp.dot((q * qdec_ref[h]).astype(BF16), s.astype(BF16), preferred_element_type=F32))
        s_new = s * cdec_ref[h] + lax.dot_general((k * kdec_ref[h]).astype(BF16), v, _TN,
                                                  preferred_element_type=F32)
        if streamed:
            sout_buf[slot, h] = s_new
            flush(b, slot, h).start()
        else:
            sout_ref[0, h] = s_new
        o = o[:chunk]
        o = o * lax.rsqrt(jnp.mean(o * o, axis=-1, keepdims=True) + GN_EPS)
        gate = gate_ref[0, :, v_cols].astype(F32)
        o_ref[0, :, v_cols] = (o * (gate * jax.nn.sigmoid(gate))).astype(o_ref.dtype)

    pending = scores(0)
    for h in range(RET_HEADS):
        following = scores(h + 1) if h + 1 < RET_HEADS else None
        finish(h, *pending)
        pending = following

    if streamed:
        @pl.when(b == n_batch - 1)
        def _():
            for h in range(RET_HEADS):
                if n_batch > 1:
                    flush(b - 1, 1 - slot, h).wait()
                flush(b, slot, h).wait()


def retention(proj, pos, s0):
    b, l, _ = proj.shape
    chunk = math.gcd(l, RET_CHUNK)
    n = l // chunk
    has_state = s0 is not None
    if not has_state:
        s0 = jnp.zeros((1, RET_HEADS, RET_DK, RET_DV), F32)
    tables = _ret_tables(pos, chunk)
    k_col, v_col, gate_col = 1, 2 * RET_QK // RET_V, 2 * RET_QK // RET_V + 1
    pos_tab = pl.BlockSpec((chunk, RET_DK), lambda bi, c: (c, 0))
    head_tab = lambda rows, cols: pl.BlockSpec((RET_HEADS, rows, cols), lambda bi, c: (0, 0, 0))
    state = lambda index_map: pl.BlockSpec((1, RET_HEADS, RET_DK, RET_DV), index_map)
    streamed = has_state and n == 1
    if streamed:
        state_in = state_out = pl.BlockSpec(memory_space=pl.ANY)
        slots = (2, RET_HEADS, RET_DK, RET_DV)
        scratch = [pltpu.VMEM(slots, F32), pltpu.VMEM(slots, F32),
                   pltpu.SemaphoreType.DMA(slots[:2]), pltpu.SemaphoreType.DMA(slots[:2])]
        semantics = ("arbitrary", "arbitrary")
    else:
        state_in = state((lambda bi, c: (bi, 0, 0, 0)) if has_state else (lambda bi, c: (0, 0, 0, 0)))
        state_out = state(lambda bi, c: (bi, 0, 0, 0))
        scratch = []
        semantics = ("parallel", "arbitrary")
    return pl.pallas_call(
        functools.partial(_retention_kernel, chunk=chunk, n_chunks=n, n_batch=b, has_state=has_state),
        grid=(b, n),
        in_specs=[pl.BlockSpec((1, chunk, RET_QK), lambda bi, c: (bi, c, 0)),
                  pl.BlockSpec((1, chunk, RET_QK), lambda bi, c: (bi, c, k_col)),
                  pl.BlockSpec((1, chunk, RET_V), lambda bi, c: (bi, c, v_col)),
                  pl.BlockSpec((1, chunk, RET_V), lambda bi, c: (bi, c, gate_col)),
                  pos_tab, pos_tab, pos_tab,
                  head_tab(RET_ROWS, RET_ROWS), head_tab(RET_ROWS, RET_DK), head_tab(RET_ROWS, RET_DK),
                  head_tab(1, RET_DV), state_in],
        out_specs=[pl.BlockSpec((1, chunk, RET_V), lambda bi, c: (bi, c, 0)), state_out],
        out_shape=[jax.ShapeDtypeStruct((b, l, RET_V), proj.dtype),
                   jax.ShapeDtypeStruct((b, RET_HEADS, RET_DK, RET_DV), F32)],
        scratch_shapes=scratch,
        compiler_params=_params(*semantics),
        name="retention",
    )(proj, proj, proj, proj, *tables, s0)


def _moba_prompt_kernel(q_ref, k_ref, v_ref, o_ref, *, nb, tile_blocks):
    k = k_ref[0]
    s_len = k.shape[0]
    means = jnp.sum(k.reshape(nb, MOBA_BLOCK, ATTN_DH), axis=1) * (1.0 / MOBA_BLOCK)
    means = _pad_rows(means, -(-nb // 8) * 8)
    key_lane = lax.broadcasted_iota(jnp.int32, (s_len, 128), 1)
    key_block = lax.broadcasted_iota(jnp.int32, (s_len, 128), 0) // MOBA_BLOCK
    kb = jnp.concatenate([k.astype(BF16), jnp.where(key_block == key_lane, 1.0, 0.0).astype(BF16)], axis=1)
    vb = jnp.concatenate([v_ref[0].astype(BF16), jnp.where(key_lane == 0, 1.0, 0.0).astype(BF16)], axis=1)
    tile = tile_blocks * MOBA_BLOCK
    n_tiles = nb // tile_blocks
    blk = lax.broadcasted_iota(jnp.int32, (means.shape[0], tile), 0)
    own_blk = lax.broadcasted_iota(jnp.int32, (means.shape[0], tile), 1) // MOBA_BLOCK
    row = lax.broadcasted_iota(jnp.int32, (tile, tile), 0)
    col = lax.broadcasted_iota(jnp.int32, (tile, tile), 1)
    causal = col <= row
    eye = jnp.where(lax.broadcasted_iota(jnp.int32, (MOBA_BLOCK, MOBA_BLOCK), 0)
                    == lax.broadcasted_iota(jnp.int32, (MOBA_BLOCK, MOBA_BLOCK), 1), 1.0, 0.0).astype(BF16)

    def masked_logits(i):
        q = q_ref[0, i * tile:(i + 1) * tile, :]
        first = i * tile_blocks
        bias = jnp.zeros((tile, 128), BF16)
        if first + tile_blocks > 1:
            s = lax.dot_general(means, q, _NT, preferred_element_type=F32, precision=lax.Precision.HIGHEST)
            elig = blk < own_blk + first
            sm = jnp.where(elig, s, NEG)
            rank = jnp.zeros(sm.shape, F32)
            for m in range(first + tile_blocks - 1):
                cm = sm[m:m + 1, :]
                beats = jnp.where(cm > sm, 1.0, jnp.where(cm == sm, jnp.where(blk > m, 1.0, 0.0), 0.0))
                rank = rank + beats
            keep = jnp.where(rank < MOBA_TOPK, jnp.where(sm > 0.5 * NEG, 0.0, NEG), NEG)
            bias_t = _pad_rows(jnp.where(elig, keep, 0.0), 128)
            bias = jnp.concatenate([
                lax.dot_general(eye, bias_t[:, j * MOBA_BLOCK:(j + 1) * MOBA_BLOCK].astype(BF16), _NT,
                                preferred_element_type=F32) for j in range(tile_blocks)], axis=0).astype(BF16)
        qa = jnp.concatenate([(q * (ATTN_DH ** -0.5 * LOG2_E)).astype(BF16), bias], axis=1)
        kv_rows = (i + 1) * tile
        logits = lax.dot_general(qa, kb[:kv_rows], _NT, preferred_element_type=F32)
        own = jnp.where(causal, logits[:, i * tile:], NEG)
        return own if i == 0 else jnp.concatenate([logits[:, :i * tile], own], axis=1)

    def attend(i, logits):
        p = jnp.exp2(logits - jnp.max(logits, axis=-1, keepdims=True))
        acc = jnp.dot(p.astype(BF16), vb[:(i + 1) * tile], preferred_element_type=F32)
        o_ref[0, i * tile:(i + 1) * tile, :] = (
            acc[:, :ATTN_DH] / acc[:, ATTN_DH:ATTN_DH + 1]).astype(o_ref.dtype)

    order = list(range(n_tiles - 1, -1, -1))
    logits = masked_logits(order[0])
    for n, i in enumerate(order):
        following = masked_logits(order[n + 1]) if n + 1 < n_tiles else None
        attend(i, logits)
        logits = following


def moba_prompt(q, k, v):
    b, s, _ = q.shape
    assert s % (MOBA_BLOCK * PROMPT_TILE_BLOCKS) == 0
    nb = s // MOBA_BLOCK
    spec = pl.BlockSpec((1, s, ATTN_DH), lambda bi, h: (bi, 0, h))
    return pl.pallas_call(
        functools.partial(_moba_prompt_kernel, nb=nb, tile_blocks=PROMPT_TILE_BLOCKS),
        grid=(b, ATTN_HEADS),
        in_specs=[spec, spec, spec],
        out_specs=spec,
        out_shape=jax.ShapeDtypeStruct((b, s, ATTN_HEADS * ATTN_DH), BF16),
        compiler_params=_params("parallel", "parallel"),
        name="moba_prompt",
    )(q, k, v)


def _block_diag_queries(q):
    l, d = q.shape
    tiled = jnp.concatenate([q] * ATTN_HEADS, axis=0)
    row_head = lax.broadcasted_iota(jnp.int32, tiled.shape, 0) // l
    col_head = lax.broadcasted_iota(jnp.int32, tiled.shape, 1) // ATTN_DH
    return jnp.where(row_head == col_head, tiled, 0.0)


def _page_rows(page_ref):
    heads = [page_ref[pl.ds(h, PAGE_SIZE, stride=ATTN_HEADS), :] for h in range(ATTN_HEADS)]
    return jnp.concatenate(heads, axis=1).astype(BF16)


class _PageStream:
    def __init__(self, pt_ref, cache_refs, buf_ref, sem_ref, n_pages):
        assert n_pages % STEP_PAGES == 0
        self.pt_ref, self.cache_refs, self.buf_ref, self.sem_ref = pt_ref, cache_refs, buf_ref, sem_ref
        self.n_pages = n_pages
        self.span = len(cache_refs) * n_pages
        self.total = pl.num_programs(0) * self.span
        self.base = pl.program_id(0) * self.span

    def _copy(self, cache_ref, page, g):
        slot = g % RING_PAGES
        return pltpu.make_async_copy(cache_ref.at[page], self.buf_ref.at[slot], self.sem_ref.at[slot])

    def _start_group(self, g0):
        seq, within = g0 // self.span, g0 % self.span
        which, first = within // self.n_pages, within % self.n_pages
        for c, cache_ref in enumerate(self.cache_refs):
            @pl.when(which == c)
            def _(cache_ref=cache_ref):
                for j in range(STEP_PAGES):
                    self._copy(cache_ref, self.pt_ref[seq, first + j], g0 + j).start()

    def prime(self):
        @pl.when(pl.program_id(0) == 0)
        def _():
            for g0 in range(0, RING_PAGES - STEP_PAGES, STEP_PAGES):
                self._start_group(jnp.int32(g0))

    def group(self, it):
        g0 = self.base + it * STEP_PAGES
        ahead = g0 + (RING_PAGES - STEP_PAGES)

        @pl.when(ahead < self.total)
        def _():
            self._start_group(ahead)

        pages = []
        for j in range(STEP_PAGES):
            self._copy(self.cache_refs[0], 0, g0 + j).wait()
            pages.append(self.buf_ref.at[(g0 + j) % RING_PAGES])
        return pages


def _moba_sample_kernel(pt_ref, q_ref, kn_ref, vn_ref, ck_ref, cv_ref, o_ref,
                        buf_ref, sem_ref, lg_ref, bsum_ref, sel_ref, m_ref, l_ref, acc_ref,
                        *, n_new, n_past_blocks, n_pages):
    stream = _PageStream(pt_ref, (ck_ref, cv_ref), buf_ref, sem_ref, n_pages)
    stream.prime()
    groups = n_pages // STEP_PAGES
    wq = _block_diag_queries(q_ref[0])
    w = (wq * ATTN_DH ** -0.5).astype(BF16)
    rows = wq.shape[0]
    lane = lax.broadcasted_iota(jnp.int32, (rows, 128), 1)

    def keys_body(it, carry):
        pages = stream.group(it)
        sums = [jnp.sum(p[...].reshape(PAGE_SIZE, ATTN_HEADS, ATTN_DH), axis=0) for p in pages]
        keys = jnp.concatenate([_page_rows(p) for p in pages], axis=0)
        logits = lax.dot_general(w, keys, _NT, preferred_element_type=F32)
        for j in range(STEP_BLOCKS):
            blk = it * STEP_BLOCKS + j
            bsum_ref[pl.ds(pl.multiple_of(blk * ATTN_HEADS, ATTN_HEADS), ATTN_HEADS), :] = functools.reduce(
                lambda x, y: x + y, sums[j * PAGES_PER_BLOCK:(j + 1) * PAGES_PER_BLOCK])
            lg_ref[blk] = logits[:, j * MOBA_BLOCK:(j + 1) * MOBA_BLOCK]
        return carry

    lax.fori_loop(0, groups, keys_body, 0)

    kn = kn_ref[0]
    past_sums = jnp.concatenate(
        [bsum_ref[pl.ds(h, n_past_blocks, stride=ATTN_HEADS), :] for h in range(ATTN_HEADS)], axis=1)
    first_row = lax.broadcasted_iota(jnp.int32, (8, kn.shape[1]), 0) == 0
    own_sum = jnp.where(first_row, jnp.sum(kn, axis=0, keepdims=True), 0.0)
    means = _pad_rows(jnp.concatenate([past_sums, own_sum], axis=0) * (1.0 / MOBA_BLOCK), 128)
    s = lax.dot_general(wq, means, _NT, preferred_element_type=F32, precision=lax.Precision.HIGHEST)
    work = jnp.where(lane < n_past_blocks, s, NEG)
    sel = jnp.zeros(work.shape, F32)
    for _ in range(MOBA_TOPK):
        mx = jnp.max(work, axis=-1, keepdims=True)
        first = jnp.min(jnp.where(work == mx, lane, 128), axis=-1, keepdims=True)
        pick = lane == first
        sel = jnp.where(pick, jnp.where(mx > 0.5 * NEG, 1.0, 0.0), sel)
        work = jnp.where(pick, -jnp.inf, work)
    sel_ref[...] = sel
    t = lax.broadcasted_iota(jnp.int32, (rows, 128), 0) % n_new
    lo = lax.dot_general(w, _pad_rows(kn, 128).astype(BF16), _NT, preferred_element_type=F32)
    lo = jnp.where(lane <= t, lo, NEG)
    m0 = jnp.max(lo, axis=-1, keepdims=True)
    p0 = jnp.exp(lo - m0)
    m_ref[...] = jnp.broadcast_to(m0, m_ref.shape)
    l_ref[...] = jnp.broadcast_to(jnp.sum(p0, axis=-1, keepdims=True), l_ref.shape)
    acc_ref[...] = jnp.dot(p0.astype(BF16), _pad_rows(vn_ref[0], 128).astype(BF16), preferred_element_type=F32)

    def values_body(it, carry):
        pages = stream.group(groups + it)
        sel = sel_ref[...]
        lgs = []
        for j in range(STEP_BLOCKS):
            blk = it * STEP_BLOCKS + j
            picked = jnp.max(jnp.where(lane == blk, sel, 0.0), axis=-1, keepdims=True) > 0.5
            lgs.append(jnp.where(picked, lg_ref[blk], NEG))
        m_old = m_ref[:, :1]
        m_new = functools.reduce(jnp.maximum, [m_old] + [jnp.max(lg, axis=-1, keepdims=True) for lg in lgs])
        alpha = jnp.exp(m_old - m_new)
        ps = [jnp.exp(lg - m_new) for lg in lgs]
        m_ref[...] = jnp.broadcast_to(m_new, m_ref.shape)
        l_ref[...] = alpha * l_ref[...] + functools.reduce(
            lambda x, y: x + y, [jnp.sum(p, axis=-1, keepdims=True) for p in ps])
        values = jnp.concatenate([_page_rows(p) for p in pages], axis=0)
        acc_ref[...] = alpha * acc_ref[...] + jnp.dot(
            jnp.concatenate(ps, axis=1).astype(BF16), values, preferred_element_type=F32)
        return carry

    lax.fori_loop(0, groups, values_body, 0)

    out = acc_ref[...] / l_ref[:, :1]
    for h in range(ATTN_HEADS):
        o_ref[0, :, h * ATTN_DH:(h + 1) * ATTN_DH] = (
            out[h * n_new:(h + 1) * n_new, h * ATTN_DH:(h + 1) * ATTN_DH].astype(o_ref.dtype))


def moba_sample(q, k_new, v_new, cache_k, cache_v, page_table):
    b, l, d = q.shape
    n_pages = PAST_LEN // PAGE_SIZE
    assert n_pages % STEP_PAGES == 0 and (PAST_LEN + l - 1) // MOBA_BLOCK == PAST_LEN // MOBA_BLOCK
    nblk = n_pages // PAGES_PER_BLOCK
    rows = ATTN_HEADS * l
    page_rows = PAGE_SIZE * ATTN_HEADS
    ck = cache_k.reshape(-1, page_rows, ATTN_DH)
    cv = cache_v.reshape(-1, page_rows, ATTN_DH)

    per_batch = lambda *shape: pl.BlockSpec((1,) + shape, lambda bi, pt: (bi,) + (0,) * len(shape))
    cache = pl.BlockSpec(memory_space=pl.ANY)
    return pl.pallas_call(
        functools.partial(_moba_sample_kernel, n_new=l, n_past_blocks=nblk, n_pages=n_pages),
        grid_spec=pltpu.PrefetchScalarGridSpec(
            num_scalar_prefetch=1,
            grid=(b,),
            in_specs=[per_batch(l, d), per_batch(l, d), per_batch(l, d), cache, cache],
            out_specs=per_batch(l, d),
            scratch_shapes=[pltpu.VMEM((RING_PAGES, page_rows, ATTN_DH), F32),
                            pltpu.SemaphoreType.DMA((RING_PAGES,)),
                            pltpu.VMEM((nblk, rows, MOBA_BLOCK), F32),
                            pltpu.VMEM((nblk * ATTN_HEADS, ATTN_DH), F32),
                            pltpu.VMEM((rows, 128), F32),
                            pltpu.VMEM((rows, 128), F32),
                            pltpu.VMEM((rows, 128), F32),
                            pltpu.VMEM((rows, d), F32)]),
        out_shape=jax.ShapeDtypeStruct((b, l, d), F32),
        compiler_params=_params("arbitrary"),
        name="moba_sample",
    )(page_table, q, k_new, v_new, ck, cv)


def kernel(x_prompt, x_sample, state_ret, cache_k, cache_v, page_table, ret_norm_g, ret_w_in, ret_w_out,
           attn_norm_g, attn_w_q, attn_w_out, kv_norm_g, w_kv, mlp_norm_g, mlp_w_up, mlp_w_down, final_norm_g):
    bp, lp, d = x_prompt.shape
    bs, ls, _ = x_sample.shape
    assert ret_w_in.shape[0] == 1 and attn_w_q.shape[0] == 1 and mlp_w_up.shape[0] == 2

    def trunk(x, pos, s0, attend):
        b, l, _ = x.shape
        h = x.reshape(b * l, d)
        act = BF16 if l % RET_ROWS == 0 else F32
        proj = norm_matmul(h, ret_norm_g[0], ret_w_in[0], act)
        o, s = retention(proj.reshape(b, l, -1), pos, s0)
        h = matmul_residual(o.reshape(b * l, RET_V).astype(BF16), ret_w_out[0], h)
        h = mlp(h, mlp_norm_g[0], mlp_w_up, mlp_w_down, 0)
        k, v, q = kvq_proj(h, kv_norm_g, attn_norm_g[0], w_kv, attn_w_q[0])
        o = attend(q.reshape(b, l, d), k.reshape(b, l, d), v.reshape(b, l, d))
        h = matmul_residual(o.reshape(b * l, d).astype(BF16), attn_w_out[0], h)
        y = mlp(h, mlp_norm_g[1], mlp_w_up, mlp_w_down, 1, g_final=final_norm_g)
        kv_shape = (b, l, ATTN_HEADS, ATTN_DH)
        return y.reshape(b, l, d), s[None], k.reshape(kv_shape), v.reshape(kv_shape)

    y_p, s_p, k_p, v_p = trunk(x_prompt, jnp.arange(lp, dtype=jnp.int32), None, moba_prompt)
    y_s, s_s, k_s, v_s = trunk(
        x_sample, PAST_LEN + jnp.arange(ls, dtype=jnp.int32), state_ret[0],
        lambda q, k, v: moba_sample(q, k, v, cache_k, cache_v, page_table))
    return (y_p, y_s, s_p, s_s, k_p, v_p, k_s, v_s)
```

```python
import functools
import math

import jax
import jax.numpy as jnp
from jax import lax
from jax.experimental import pallas as pl
from jax.experimental.pallas import tpu as pltpu

F32 = jnp.float32
BF16 = jnp.bfloat16

D_MODEL = 1024
PAST_LEN = 8192
PAGE_SIZE = 128
RET_HEADS = 4
RET_DK = D_MODEL // RET_HEADS
RET_QK = RET_HEADS * RET_DK
RET_V = 2 * D_MODEL
RET_DV = RET_V // RET_HEADS
RET_CHUNK = 128
ATTN_HEADS = 8
ATTN_DH = D_MODEL // ATTN_HEADS
MOBA_BLOCK = 256
MOBA_TOPK = 3
D_FF = 4 * D_MODEL
NORM_EPS = 1e-5
GN_EPS = 1e-6
NEG = -1e30
LOG2_E = math.log2(math.e)

PAGES_PER_BLOCK = MOBA_BLOCK // PAGE_SIZE
STEP_PAGES = 8
STEP_BLOCKS = STEP_PAGES // PAGES_PER_BLOCK
RING_PAGES = 3 * STEP_PAGES
PROMPT_TILE_BLOCKS = 2
ROW_CHUNKS = 4
WEIGHT_SPLITS = 4
RET_ROWS = 128
VMEM_LIMIT_BYTES = 56 * 1024 * 1024

_NT = (((1,), (1,)), ((), ()))
_TN = (((0,), (0,)), ((), ()))


def _params(*sem):
    return pltpu.CompilerParams(dimension_semantics=sem, vmem_limit_bytes=VMEM_LIMIT_BYTES)


def _rms(x, g, eps=NORM_EPS):
    return x * lax.rsqrt(jnp.mean(x * x, axis=-1, keepdims=True) + eps) * g


def _row_chunks(rows, chunks=ROW_CHUNKS):
    size = rows // chunks if rows % (chunks * 16) == 0 else rows
    return [slice(r, r + size) for r in range(0, rows, size)]


def _cast_rows(w_refs, wb_ref):
    rows = w_refs[0].shape[0]
    for s, w_ref in enumerate(w_refs):
        wb_ref[s * rows:(s + 1) * rows, :] = w_ref[...].astype(BF16)


def _norm_matmul_kernel(x_ref, g_ref, *refs):
    w_refs, (o_ref, wb_ref) = refs[:WEIGHT_SPLITS], refs[WEIGHT_SPLITS:]

    @pl.when(pl.program_id(1) == 0)
    def _():
        _cast_rows(w_refs, wb_ref)

    for rows in _row_chunks(x_ref.shape[0]):
        xn = _rms(x_ref[rows, :], g_ref[...]).astype(BF16)
        o_ref[rows, :] = jnp.dot(xn, wb_ref[...], preferred_element_type=F32).astype(o_ref.dtype)


def norm_matmul(x, g, w, out_dtype, tm=1024, tn=2048):
    m, d = x.shape
    n = w.shape[1]
    tm = min(tm, m)
    return pl.pallas_call(
        _norm_matmul_kernel,
        grid=(n // tn, m // tm),
        in_specs=[pl.BlockSpec((tm, d), lambda j, i: (i, 0)),
                  pl.BlockSpec((1, d), lambda j, i: (0, 0))]
                 + [pl.BlockSpec((d // WEIGHT_SPLITS, tn), functools.partial(lambda j, i, s: (s, j), s=s))
                    for s in range(WEIGHT_SPLITS)],
        out_specs=pl.BlockSpec((tm, tn), lambda j, i: (i, j)),
        out_shape=jax.ShapeDtypeStruct((m, n), out_dtype),
        scratch_shapes=[pltpu.VMEM((d, tn), BF16)],
        compiler_params=_params("parallel", "arbitrary"),
        name="norm_matmul",
    )(x, g.reshape(1, d), *([w] * WEIGHT_SPLITS))


def _resident_weight(shape):
    rows, cols = shape
    return [pl.BlockSpec((rows // WEIGHT_SPLITS, cols), functools.partial(lambda i, s: (s, 0), s=s),
                         pipeline_mode=pl.Buffered(1)) for s in range(WEIGHT_SPLITS)]


def _matmul_residual_kernel(a_ref, *refs):
    w_refs, (r_ref, o_ref, wb_ref) = refs[:WEIGHT_SPLITS], refs[WEIGHT_SPLITS:]

    @pl.when(pl.program_id(0) == 0)
    def _():
        _cast_rows(w_refs, wb_ref)

    o_ref[...] = r_ref[...] + jnp.dot(a_ref[...], wb_ref[...], preferred_element_type=F32)


def matmul_residual(a, w, res, tm=1024):
    m, k = a.shape
    n = w.shape[1]
    tm = min(tm, m)
    return pl.pallas_call(
        _matmul_residual_kernel,
        grid=(m // tm,),
        in_specs=[pl.BlockSpec((tm, k), lambda i: (i, 0)),
                  *_resident_weight((k, n)),
                  pl.BlockSpec((tm, n), lambda i: (i, 0))],
        out_specs=pl.BlockSpec((tm, n), lambda i: (i, 0)),
        out_shape=jax.ShapeDtypeStruct((m, n), F32),
        scratch_shapes=[pltpu.VMEM((k, n), BF16)],
        compiler_params=_params("arbitrary"),
        name="matmul_residual",
    )(a, *([w] * WEIGHT_SPLITS), res)


def _mlp_kernel(x_ref, g_ref, wu_ref, wd_ref, gf_ref, o_ref, xn_ref, acc_ref, *, final_norm):
    f = pl.program_id(1)
    last = pl.num_programs(1) - 1

    def weights():
        return wu_ref[...].astype(BF16), wd_ref[...].astype(BF16)

    def hidden(xn, wu, wd):
        u = jnp.maximum(jnp.dot(xn, wu, preferred_element_type=F32), 0.0)
        return jnp.dot((u * u).astype(BF16), wd, preferred_element_type=F32)

    @pl.when(f == 0)
    def _():
        wu, wd = weights()
        for rows in _row_chunks(x_ref.shape[0]):
            xn = _rms(x_ref[rows, :], g_ref[...]).astype(BF16)
            xn_ref[rows, :] = xn
            acc_ref[rows, :] = hidden(xn, wu, wd)

    @pl.when(jnp.logical_and(f != 0, f != last))
    def _():
        acc_ref[...] += hidden(xn_ref[...], *weights())

    @pl.when(f == last)
    def _():
        wu, wd = weights()
        for rows in _row_chunks(x_ref.shape[0]):
            h = x_ref[rows, :] + acc_ref[rows, :] + hidden(xn_ref[rows, :], wu, wd)
            o_ref[rows, :] = _rms(h, gf_ref[...]) if final_norm else h


def mlp(x, g, w_up, w_down, layer, g_final=None, tm=1024, tf=1024):
    m, d = x.shape
    ff = w_up.shape[2]
    assert ff // tf >= 2
    tm = min(tm, m)
    final_norm = g_final is not None
    gf = g_final if final_norm else g
    return pl.pallas_call(
        functools.partial(_mlp_kernel, final_norm=final_norm),
        grid=(m // tm, ff // tf),
        in_specs=[pl.BlockSpec((tm, d), lambda i, f: (i, 0)),
                  pl.BlockSpec((1, d), lambda i, f: (0, 0)),
                  pl.BlockSpec((None, d, tf), lambda i, f: (layer, 0, f)),
                  pl.BlockSpec((None, tf, d), lambda i, f: (layer, f, 0)),
                  pl.BlockSpec((1, d), lambda i, f: (0, 0))],
        out_specs=pl.BlockSpec((tm, d), lambda i, f: (i, 0)),
        out_shape=jax.ShapeDtypeStruct((m, d), F32),
        scratch_shapes=[pltpu.VMEM((tm, d), BF16), pltpu.VMEM((tm, d), F32)],
        compiler_params=_params("parallel", "arbitrary"),
        name="mlp",
    )(x, g.reshape(1, d), w_up, w_down, gf.reshape(1, d))


def _kvq_kernel(x_ref, gkv_ref, gq_ref, *refs):
    wkv_refs, wq_refs = refs[:WEIGHT_SPLITS], refs[WEIGHT_SPLITS:2 * WEIGHT_SPLITS]
    k_ref, v_ref, q_ref, wkvb_ref, wqb_ref = refs[2 * WEIGHT_SPLITS:]

    @pl.when(pl.program_id(0) == 0)
    def _():
        _cast_rows(wkv_refs, wkvb_ref)
        _cast_rows(wq_refs, wqb_ref)

    d = x_ref.shape[1]
    for rows in _row_chunks(x_ref.shape[0], 2):
        x = x_ref[rows, :]
        r = x * lax.rsqrt(jnp.mean(x * x, axis=-1, keepdims=True) + NORM_EPS)
        xkv = (r * gkv_ref[...]).astype(BF16)
        xq = (r * gq_ref[...]).astype(BF16)
        k_ref[rows, :] = jnp.dot(xkv, wkvb_ref[:, :d], preferred_element_type=F32)
        v_ref[rows, :] = jnp.dot(xkv, wkvb_ref[:, d:], preferred_element_type=F32)
        q_ref[rows, :] = jnp.dot(xq, wqb_ref[...], preferred_element_type=F32)


def kvq_proj(x, g_kv, g_q, w_kv, w_q, tm=512):
    m, d = x.shape
    tm = min(tm, m)
    row = pl.BlockSpec((tm, d), lambda i: (i, 0))
    gain = pl.BlockSpec((1, d), lambda i: (0, 0))
    out = jax.ShapeDtypeStruct((m, d), F32)
    return pl.pallas_call(
        _kvq_kernel,
        grid=(m // tm,),
        in_specs=[row, gain, gain, *_resident_weight((d, 2 * d)), *_resident_weight((d, d))],
        out_specs=[row, row, row],
        out_shape=[out, out, out],
        scratch_shapes=[pltpu.VMEM((d, 2 * d), BF16), pltpu.VMEM((d, d), BF16)],
        compiler_params=_params("arbitrary"),
        name="kvq_proj",
    )(x, g_kv.reshape(1, d), g_q.reshape(1, d), *([w_kv] * WEIGHT_SPLITS), *([w_q] * WEIGHT_SPLITS))


def _ret_tables(first_pos, length, chunk):
    angle = 1.0 / (10000.0 ** jnp.linspace(0.0, 1.0, RET_DK // 2, dtype=F32))
    angle = jnp.repeat(angle, 2)
    pos = first_pos + jnp.arange(length, dtype=jnp.int32)
    ang = pos.astype(F32)[:, None] * angle[None, :]
    sin, cos = jnp.sin(ang), jnp.cos(ang)
    even = (jnp.arange(RET_DK) % 2 == 0)[None, :]
    sin_next = jnp.where(even, -sin, 0.0)
    sin_prev = jnp.where(even, 0.0, sin)

    lg = jnp.log(1.0 - 2.0 ** (-5.0 - jnp.arange(RET_HEADS, dtype=F32)))
    i = jnp.arange(chunk, dtype=F32)
    diff = i[:, None] - i[None, :]
    d_intra = jnp.exp(jnp.where(diff >= 0, lg[:, None, None] * diff, -jnp.inf))
    q_dec = jnp.exp(lg[:, None] * (i + 1.0))
    k_dec = jnp.exp(lg[:, None] * (chunk - 1.0 - i))
    c_dec = jnp.exp(lg * chunk)
    d_intra = d_intra * RET_DK ** -0.5
    k_dec = k_dec * RET_DK ** -0.5
    pad = RET_ROWS - chunk
    d_intra = jnp.pad(d_intra, ((0, 0), (0, pad), (0, pad)))
    q_dec = jnp.broadcast_to(jnp.pad(q_dec, ((0, 0), (0, pad)))[:, :, None], (RET_HEADS, RET_ROWS, RET_DK))
    k_dec = jnp.broadcast_to(jnp.pad(k_dec, ((0, 0), (0, pad)))[:, :, None], (RET_HEADS, RET_ROWS, RET_DK))
    c_dec = jnp.broadcast_to(c_dec[:, None, None], (RET_HEADS, 1, RET_DV))
    return cos, sin_next, sin_prev, d_intra, q_dec, k_dec, c_dec


def _theta_shift(x, cos, sin_next, sin_prev):
    dk = x.shape[-1]
    return x * cos + pltpu.roll(x, dk - 1, 1) * sin_next + pltpu.roll(x, 1, 1) * sin_prev


def _pad_rows(x, rows):
    if x.shape[0] == rows:
        return x
    return jnp.concatenate([x, jnp.zeros((rows - x.shape[0], x.shape[1]), x.dtype)], axis=0)


def _retention_kernel(q_ref, k_ref, v_ref, gate_ref, cos_ref, sn_ref, sp_ref, dintra_ref, qdec_ref, kdec_ref,
                      cdec_ref, s0_ref, o_ref, sout_ref, *stream, chunk, n_chunks, n_batch, has_state):
    seeded = n_chunks > 1
    streamed = bool(stream)
    if seeded:
        @pl.when(pl.program_id(1) == 0)
        def _():
            sout_ref[0] = s0_ref[0] if has_state else jnp.zeros(sout_ref.shape[1:], F32)
    if streamed:
        sin_buf, sout_buf, sem_in, sem_out = stream
        b = pl.program_id(0)
        slot = b % 2

        def fetch(bb, sl, h):
            return pltpu.make_async_copy(s0_ref.at[bb, h], sin_buf.at[sl, h], sem_in.at[sl, h])

        def flush(bb, sl, h):
            return pltpu.make_async_copy(sout_buf.at[sl, h], sout_ref.at[bb, h], sem_out.at[sl, h])

        @pl.when(b == 0)
        def _():
            for h in range(RET_HEADS):
                fetch(0, 0, h).start()

        @pl.when(b + 1 < n_batch)
        def _():
            for h in range(RET_HEADS):
                fetch(b + 1, 1 - slot, h).start()

        @pl.when(b >= 2)
        def _():
            for h in range(RET_HEADS):
                flush(b - 2, slot, h).wait()

    cos, sn, sp = cos_ref[...], sn_ref[...], sp_ref[...]

    def scores(h):
        qk_cols = slice(h * RET_DK, (h + 1) * RET_DK)
        q = _theta_shift(q_ref[0, :, qk_cols].astype(F32), cos, sn, sp)
        k = _theta_shift(k_ref[0, :, qk_cols].astype(F32), cos, sn, sp)
        q = _pad_rows(q, RET_ROWS)
        k = _pad_rows(k, RET_ROWS)
        a = lax.dot_general(q.astype(BF16), k.astype(BF16), _NT, preferred_element_type=F32) * dintra_ref[h]
        return q, k, a

    def finish(h, q, k, a):
        v_cols = slice(h * RET_DV, (h + 1) * RET_DV)
        v = _pad_rows(v_ref[0, :, v_cols], RET_ROWS).astype(BF16)
        if streamed:
            fetch(b, slot, h).wait()
            s = sin_buf[slot, h]
        elif seeded:
            s = sout_ref[0, h]
        else:
            s = s0_ref[0, h] if has_state else jnp.zeros((RET_DK, RET_DV), F32)
        o = (jnp.dot(a.astype(BF16), v, preferred_element_type=F32)
             + jnp.dot((q * qdec_ref[h]).astype(BF16), s.astype(BF16), preferred_element_type=F32))
        s_new = s * cdec_ref[h] + lax.dot_general((k * kdec_ref[h]).astype(BF16), v, _TN,
                                                  preferred_element_type=F32)
        if streamed:
            sout_buf[slot, h] = s_new
            flush(b, slot, h).start()
        else:
            sout_ref[0, h] = s_new
        return o[:chunk]

    def gate_out(h, o):
        v_cols = slice(h * RET_DV, (h + 1) * RET_DV)
        o = o * lax.rsqrt(jnp.mean(o * o, axis=-1, keepdims=True) + GN_EPS)
        gate = gate_ref[0, :, v_cols].astype(F32)
        o_ref[0, :, v_cols] = (o * (gate * jax.nn.sigmoid(gate))).astype(o_ref.dtype)

    stage1 = {0: scores(0)}
    stage2 = {}
    for h in range(RET_HEADS + 1):
        if h + 1 < RET_HEADS:
            stage1[h + 1] = scores(h + 1)
        if h < RET_HEADS:
            stage2[h] = finish(h, *stage1.pop(h))
        if h >= 1:
            gate_out(h - 1, stage2.pop(h - 1))

    if streamed:
        @pl.when(b == n_batch - 1)
        def _():
            for h in range(RET_HEADS):
                if n_batch > 1:
                    flush(b - 1, 1 - slot, h).wait()
                flush(b, slot, h).wait()


def retention(proj, first_pos, s0):
    b, l, _ = proj.shape
    chunk = math.gcd(l, RET_CHUNK)
    n = l // chunk
    has_state = s0 is not None
    if not has_state:
        s0 = jnp.zeros((1, RET_HEADS, RET_DK, RET_DV), F32)
    tables = _ret_tables(first_pos, l, chunk)
    k_col, v_col, gate_col = 1, 2 * RET_QK // RET_V, 2 * RET_QK // RET_V + 1
    pos_tab = pl.BlockSpec((chunk, RET_DK), lambda bi, c: (c, 0))
    head_tab = lambda rows, cols: pl.BlockSpec((RET_HEADS, rows, cols), lambda bi, c: (0, 0, 0))
    state = lambda index_map: pl.BlockSpec((1, RET_HEADS, RET_DK, RET_DV), index_map)
    streamed = has_state and n == 1
    if streamed:
        state_in = state_out = pl.BlockSpec(memory_space=pl.ANY)
        slots = (2, RET_HEADS, RET_DK, RET_DV)
        scratch = [pltpu.VMEM(slots, F32), pltpu.VMEM(slots, F32),
                   pltpu.SemaphoreType.DMA(slots[:2]), pltpu.SemaphoreType.DMA(slots[:2])]
        semantics = ("arbitrary", "arbitrary")
    else:
        state_in = state((lambda bi, c: (bi, 0, 0, 0)) if has_state else (lambda bi, c: (0, 0, 0, 0)))
        state_out = state(lambda bi, c: (bi, 0, 0, 0))
        scratch = []
        semantics = ("parallel", "arbitrary")
    return pl.pallas_call(
        functools.partial(_retention_kernel, chunk=chunk, n_chunks=n, n_batch=b, has_state=has_state),
        grid=(b, n),
        in_specs=[pl.BlockSpec((1, chunk, RET_QK), lambda bi, c: (bi, c, 0)),
                  pl.BlockSpec((1, chunk, RET_QK), lambda bi, c: (bi, c, k_col)),
                  pl.BlockSpec((1, chunk, RET_V), lambda bi, c: (bi, c, v_col)),
                  pl.BlockSpec((1, chunk, RET_V), lambda bi, c: (bi, c, gate_col)),
                  pos_tab, pos_tab, pos_tab,
                  head_tab(RET_ROWS, RET_ROWS), head_tab(RET_ROWS, RET_DK), head_tab(RET_ROWS, RET_DK),
                  head_tab(1, RET_DV), state_in],
        out_specs=[pl.BlockSpec((1, chunk, RET_V), lambda bi, c: (bi, c, 0)), state_out],
        out_shape=[jax.ShapeDtypeStruct((b, l, RET_V), proj.dtype),
                   jax.ShapeDtypeStruct((b, RET_HEADS, RET_DK, RET_DV), F32)],
        scratch_shapes=scratch,
        compiler_params=_params(*semantics),
        name="retention",
    )(proj, proj, proj, proj, *tables, s0)


def _moba_prompt_kernel(q_ref, k_ref, v_ref, o_ref, *, nb, tile_blocks):
    k = k_ref[0]
    s_len = k.shape[0]
    means = jnp.sum(k.reshape(nb, MOBA_BLOCK, ATTN_DH), axis=1) * (1.0 / MOBA_BLOCK)
    means = _pad_rows(means, -(-nb // 8) * 8)
    key_lane = lax.broadcasted_iota(jnp.int32, (s_len, 128), 1)
    key_block = lax.broadcasted_iota(jnp.int32, (s_len, 128), 0) // MOBA_BLOCK
    kb = jnp.concatenate([k.astype(BF16), jnp.where(key_block == key_lane, 1.0, 0.0).astype(BF16)], axis=1)
    vb = jnp.concatenate([v_ref[0].astype(BF16), jnp.where(key_lane == 0, 1.0, 0.0).astype(BF16)], axis=1)
    tile = tile_blocks * MOBA_BLOCK
    n_tiles = nb // tile_blocks
    blk = lax.broadcasted_iota(jnp.int32, (means.shape[0], tile), 0)
    own_blk = lax.broadcasted_iota(jnp.int32, (means.shape[0], tile), 1) // MOBA_BLOCK
    row = lax.broadcasted_iota(jnp.int32, (tile, tile), 0)
    col = lax.broadcasted_iota(jnp.int32, (tile, tile), 1)
    causal = col <= row
    eye = jnp.where(lax.broadcasted_iota(jnp.int32, (MOBA_BLOCK, MOBA_BLOCK), 0)
                    == lax.broadcasted_iota(jnp.int32, (MOBA_BLOCK, MOBA_BLOCK), 1), 1.0, 0.0).astype(BF16)

    def masked_logits(i):
        q = q_ref[0, i * tile:(i + 1) * tile, :]
        first = i * tile_blocks
        bias = jnp.zeros((tile, 128), BF16)
        if first + tile_blocks > 1:
            s = lax.dot_general(means, q, _NT, preferred_element_type=F32, precision=lax.Precision.HIGHEST)
            elig = blk < own_blk + first
            sm = jnp.where(elig, s, NEG)
            rank = jnp.zeros(sm.shape, F32)
            for m in range(first + tile_blocks - 1):
                cm = sm[m:m + 1, :]
                beats = jnp.where(cm > sm, 1.0, jnp.where(cm == sm, jnp.where(blk > m, 1.0, 0.0), 0.0))
                rank = rank + beats
            keep = jnp.where(rank < MOBA_TOPK, jnp.where(sm > 0.5 * NEG, 0.0, NEG), NEG)
            bias_t = _pad_rows(jnp.where(elig, keep, 0.0), 128)
            bias = jnp.concatenate([
                lax.dot_general(eye, bias_t[:, j * MOBA_BLOCK:(j + 1) * MOBA_BLOCK].astype(BF16), _NT,
                                preferred_element_type=F32) for j in range(tile_blocks)], axis=0).astype(BF16)
        qa = jnp.concatenate([(q * (ATTN_DH ** -0.5 * LOG2_E)).astype(BF16), bias], axis=1)
        kv_rows = (i + 1) * tile
        logits = lax.dot_general(qa, kb[:kv_rows], _NT, preferred_element_type=F32)
        own = jnp.where(causal, logits[:, i * tile:], NEG)
        return own if i == 0 else jnp.concatenate([logits[:, :i * tile], own], axis=1)

    def attend(i, logits):
        p = jnp.exp2(logits - jnp.max(logits, axis=-1, keepdims=True))
        acc = jnp.dot(p.astype(BF16), vb[:(i + 1) * tile], preferred_element_type=F32)
        o_ref[0, i * tile:(i + 1) * tile, :] = (
            acc[:, :ATTN_DH] / acc[:, ATTN_DH:ATTN_DH + 1]).astype(o_ref.dtype)

    order = list(range(n_tiles - 1, -1, -1))
    logits = masked_logits(order[0])
    for n, i in enumerate(order):
        following = masked_logits(order[n + 1]) if n + 1 < n_tiles else None
        attend(i, logits)
        logits = following


def moba_prompt(q, k, v):
    b, s, _ = q.shape
    assert s % (MOBA_BLOCK * PROMPT_TILE_BLOCKS) == 0
    nb = s // MOBA_BLOCK
    spec = pl.BlockSpec((1, s, ATTN_DH), lambda bi, h: (bi, 0, h))
    return pl.pallas_call(
        functools.partial(_moba_prompt_kernel, nb=nb, tile_blocks=PROMPT_TILE_BLOCKS),
        grid=(b, ATTN_HEADS),
        in_specs=[spec, spec, spec],
        out_specs=spec,
        out_shape=jax.ShapeDtypeStruct((b, s, ATTN_HEADS * ATTN_DH), BF16),
        compiler_params=_params("parallel", "parallel"),
        name="moba_prompt",
    )(q, k, v)


def _block_diag_queries(q):
    l, d = q.shape
    tiled = jnp.concatenate([q] * ATTN_HEADS, axis=0)
    row_head = lax.broadcasted_iota(jnp.int32, tiled.shape, 0) // l
    col_head = lax.broadcasted_iota(jnp.int32, tiled.shape, 1) // ATTN_DH
    return jnp.where(row_head == col_head, tiled, 0.0)


def _page_rows(page_ref):
    heads = [page_ref[pl.ds(h, PAGE_SIZE, stride=ATTN_HEADS), :] for h in range(ATTN_HEADS)]
    return jnp.concatenate(heads, axis=1).astype(BF16)


class _PageStream:
    def __init__(self, pt_ref, cache_refs, buf_ref, sem_ref, n_pages):
        assert n_pages % STEP_PAGES == 0
        self.pt_ref, self.cache_refs, self.buf_ref, self.sem_ref = pt_ref, cache_refs, buf_ref, sem_ref
        self.n_pages = n_pages
        self.span = len(cache_refs) * n_pages
        self.total = pl.num_programs(0) * self.span
        self.base = pl.program_id(0) * self.span

    def _copy(self, cache_ref, page, g):
        slot = g % RING_PAGES
        return pltpu.make_async_copy(cache_ref.at[page], self.buf_ref.at[slot], self.sem_ref.at[slot])

    def _start_group(self, g0):
        seq, within = g0 // self.span, g0 % self.span
        which, first = within // self.n_pages, within % self.n_pages
        for c, cache_ref in enumerate(self.cache_refs):
            @pl.when(which == c)
            def _(cache_ref=cache_ref):
                for j in range(STEP_PAGES):
                    self._copy(cache_ref, self.pt_ref[seq, first + j], g0 + j).start()

    def prime(self):
        @pl.when(pl.program_id(0) == 0)
        def _():
            for g0 in range(0, RING_PAGES - STEP_PAGES, STEP_PAGES):
                self._start_group(jnp.int32(g0))

    def group(self, it):
        g0 = self.base + it * STEP_PAGES
        ahead = g0 + (RING_PAGES - STEP_PAGES)

        @pl.when(ahead < self.total)
        def _():
            self._start_group(ahead)

        pages = []
        for j in range(STEP_PAGES):
            self._copy(self.cache_refs[0], 0, g0 + j).wait()
            pages.append(self.buf_ref.at[(g0 + j) % RING_PAGES])
        return pages


def _moba_sample_kernel(pt_ref, q_ref, kn_ref, vn_ref, ck_ref, cv_ref, o_ref,
                        buf_ref, sem_ref, lg_ref, bsum_ref, sel_ref, m_ref, l_ref, acc_ref,
                        *, n_new, n_past_blocks, n_pages):
    stream = _PageStream(pt_ref, (ck_ref, cv_ref), buf_ref, sem_ref, n_pages)
    stream.prime()
    groups = n_pages // STEP_PAGES
    wq = _block_diag_queries(q_ref[0])
    w = (wq * ATTN_DH ** -0.5).astype(BF16)
    rows = wq.shape[0]
    lane = lax.broadcasted_iota(jnp.int32, (rows, 128), 1)

    def keys_body(it, carry):
        pages = stream.group(it)
        sums = [jnp.sum(p[...].reshape(PAGE_SIZE, ATTN_HEADS, ATTN_DH), axis=0) for p in pages]
        keys = jnp.concatenate([_page_rows(p) for p in pages], axis=0)
        logits = lax.dot_general(w, keys, _NT, preferred_element_type=F32)
        for j in range(STEP_BLOCKS):
            blk = it * STEP_BLOCKS + j
            bsum_ref[pl.ds(pl.multiple_of(blk * ATTN_HEADS, ATTN_HEADS), ATTN_HEADS), :] = functools.reduce(
                lambda x, y: x + y, sums[j * PAGES_PER_BLOCK:(j + 1) * PAGES_PER_BLOCK])
            lg_ref[blk] = logits[:, j * MOBA_BLOCK:(j + 1) * MOBA_BLOCK]
        return carry

    lax.fori_loop(0, groups, keys_body, 0)

    kn = kn_ref[0]
    past_sums = jnp.concatenate(
        [bsum_ref[pl.ds(h, n_past_blocks, stride=ATTN_HEADS), :] for h in range(ATTN_HEADS)], axis=1)
    first_row = lax.broadcasted_iota(jnp.int32, (8, kn.shape[1]), 0) == 0
    own_sum = jnp.where(first_row, jnp.sum(kn, axis=0, keepdims=True), 0.0)
    means = _pad_rows(jnp.concatenate([past_sums, own_sum], axis=0) * (1.0 / MOBA_BLOCK), 128)
    s = lax.dot_general(wq, means, _NT, preferred_element_type=F32, precision=lax.Precision.HIGHEST)
    work = jnp.where(lane < n_past_blocks, s, NEG)
    sel = jnp.zeros(work.shape, F32)
    for _ in range(MOBA_TOPK):
        mx = jnp.max(work, axis=-1, keepdims=True)
        first = jnp.min(jnp.where(work == mx, lane, 128), axis=-1, keepdims=True)
        pick = lane == first
        sel = jnp.where(pick, jnp.where(mx > 0.5 * NEG, 1.0, 0.0), sel)
        work = jnp.where(pick, -jnp.inf, work)
    sel_ref[...] = sel
    t = lax.broadcasted_iota(jnp.int32, (rows, 128), 0) % n_new
    lo = lax.dot_general(w, _pad_rows(kn, 128).astype(BF16), _NT, preferred_element_type=F32)
    lo = jnp.where(lane <= t, lo, NEG)
    m0 = jnp.max(lo, axis=-1, keepdims=True)
    p0 = jnp.exp(lo - m0)
    m_ref[...] = jnp.broadcast_to(m0, m_ref.shape)
    l_ref[...] = jnp.broadcast_to(jnp.sum(p0, axis=-1, keepdims=True), l_ref.shape)
    acc_ref[...] = jnp.dot(p0.astype(BF16), _pad_rows(vn_ref[0], 128).astype(BF16), preferred_element_type=F32)

    def values_body(it, carry):
        pages = stream.group(groups + it)
        sel = sel_ref[...]
        lgs = []
        for j in range(STEP_BLOCKS):
            blk = it * STEP_BLOCKS + j
            picked = jnp.max(jnp.where(lane == blk, sel, 0.0), axis=-1, keepdims=True) > 0.5
            lgs.append(jnp.where(picked, lg_ref[blk], NEG))
        m_old = m_ref[:, :1]
        m_new = functools.reduce(jnp.maximum, [m_old] + [jnp.max(lg, axis=-1, keepdims=True) for lg in lgs])
        alpha = jnp.exp(m_old - m_new)
        ps = [jnp.exp(lg - m_new) for lg in lgs]
        m_ref[...] = jnp.broadcast_to(m_new, m_ref.shape)
        l_ref[...] = alpha * l_ref[...] + functools.reduce(
            lambda x, y: x + y, [jnp.sum(p, axis=-1, keepdims=True) for p in ps])
        values = jnp.concatenate([_page_rows(p) for p in pages], axis=0)
        acc_ref[...] = alpha * acc_ref[...] + jnp.dot(
            jnp.concatenate(ps, axis=1).astype(BF16), values, preferred_element_type=F32)
        return carry

    lax.fori_loop(0, groups, values_body, 0)

    out = acc_ref[...] / l_ref[:, :1]
    for h in range(ATTN_HEADS):
        o_ref[0, :, h * ATTN_DH:(h + 1) * ATTN_DH] = (
            out[h * n_new:(h + 1) * n_new, h * ATTN_DH:(h + 1) * ATTN_DH].astype(o_ref.dtype))


def moba_sample(q, k_new, v_new, cache_k, cache_v, page_table):
    b, l, d = q.shape
    n_pages = PAST_LEN // PAGE_SIZE
    assert n_pages % STEP_PAGES == 0 and (PAST_LEN + l - 1) // MOBA_BLOCK == PAST_LEN // MOBA_BLOCK
    nblk = n_pages // PAGES_PER_BLOCK
    rows = ATTN_HEADS * l
    page_rows = PAGE_SIZE * ATTN_HEADS
    ck = cache_k.reshape(-1, page_rows, ATTN_DH)
    cv = cache_v.reshape(-1, page_rows, ATTN_DH)

    per_batch = lambda *shape: pl.BlockSpec((1,) + shape, lambda bi, pt: (bi,) + (0,) * len(shape))
    cache = pl.BlockSpec(memory_space=pl.ANY)
    return pl.pallas_call(
        functools.partial(_moba_sample_kernel, n_new=l, n_past_blocks=nblk, n_pages=n_pages),
        grid_spec=pltpu.PrefetchScalarGridSpec(
            num_scalar_prefetch=1,
            grid=(b,),
            in_specs=[per_batch(l, d), per_batch(l, d), per_batch(l, d), cache, cache],
            out_specs=per_batch(l, d),
            scratch_shapes=[pltpu.VMEM((RING_PAGES, page_rows, ATTN_DH), F32),
                            pltpu.SemaphoreType.DMA((RING_PAGES,)),
                            pltpu.VMEM((nblk, rows, MOBA_BLOCK), F32),
                            pltpu.VMEM((nblk * ATTN_HEADS, ATTN_DH), F32),
                            pltpu.VMEM((rows, 128), F32),
                            pltpu.VMEM((rows, 128), F32),
                            pltpu.VMEM((rows, 128), F32),
                            pltpu.VMEM((rows, d), F32)]),
        out_shape=jax.ShapeDtypeStruct((b, l, d), F32),
        compiler_params=_params("arbitrary"),
        name="moba_sample",
    )(page_table, q, k_new, v_new, ck, cv)


def kernel(x_prompt, x_sample, state_ret, cache_k, cache_v, page_table, ret_norm_g, ret_w_in, ret_w_out,
           attn_norm_g, attn_w_q, attn_w_out, kv_norm_g, w_kv, mlp_norm_g, mlp_w_up, mlp_w_down, final_norm_g):
    bp, lp, d = x_prompt.shape
    bs, ls, _ = x_sample.shape
    assert ret_w_in.shape[0] == 1 and attn_w_q.shape[0] == 1 and mlp_w_up.shape[0] == 2

    def trunk(x, pos, s0, attend):
        b, l, _ = x.shape
        h = x.reshape(b * l, d)
        act = BF16 if l % RET_ROWS == 0 else F32
        proj = norm_matmul(h, ret_norm_g[0], ret_w_in[0], act)
        o, s = retention(proj.reshape(b, l, -1), pos, s0)
        h = matmul_residual(o.reshape(b * l, RET_V).astype(BF16), ret_w_out[0], h)
        h = mlp(h, mlp_norm_g[0], mlp_w_up, mlp_w_down, 0)
        k, v, q = kvq_proj(h, kv_norm_g, attn_norm_g[0], w_kv, attn_w_q[0])
        o = attend(q.reshape(b, l, d), k.reshape(b, l, d), v.reshape(b, l, d))
        h = matmul_residual(o.reshape(b * l, d).astype(BF16), attn_w_out[0], h)
        y = mlp(h, mlp_norm_g[1], mlp_w_up, mlp_w_down, 1, g_final=final_norm_g)
        kv_shape = (b, l, ATTN_HEADS, ATTN_DH)
        return y.reshape(b, l, d), s[None], k.reshape(kv_shape), v.reshape(kv_shape)

    y_p, s_p, k_p, v_p = trunk(x_prompt, 0, None, moba_prompt)
    y_s, s_s, k_s, v_s = trunk(
        x_sample, PAST_LEN, state_ret[0],
        lambda q, k, v: moba_sample(q, k, v, cache_k, cache_v, page_table))
    return (y_p, y_s, s_p, s_s, k_p, v_p, k_s, v_s)
```

```python
import functools
import math

import jax
import jax.numpy as jnp
from jax import lax
from jax.experimental import pallas as pl
from jax.experimental.pallas import tpu as pltpu

F32 = jnp.float32
BF16 = jnp.bfloat16

D_MODEL = 1024
PAST_LEN = 8192
PAGE_SIZE = 128
RET_HEADS = 4
RET_DK = D_MODEL // RET_HEADS
RET_QK = RET_HEADS * RET_DK
RET_V = 2 * D_MODEL
RET_DV = RET_V // RET_HEADS
RET_CHUNK = 128
ATTN_HEADS = 8
ATTN_DH = D_MODEL // ATTN_HEADS
MOBA_BLOCK = 256
MOBA_TOPK = 3
D_FF = 4 * D_MODEL
NORM_EPS = 1e-5
GN_EPS = 1e-6
NEG = -1e30
LOG2_E = math.log2(math.e)

PAGES_PER_BLOCK = MOBA_BLOCK // PAGE_SIZE
STEP_PAGES = 8
STEP_BLOCKS = STEP_PAGES // PAGES_PER_BLOCK
RING_PAGES = 3 * STEP_PAGES
PROMPT_TILE_BLOCKS = 2
ROW_CHUNKS = 4
WEIGHT_SPLITS = 4
RET_ROWS = 128
VMEM_LIMIT_BYTES = 56 * 1024 * 1024

_NT = (((1,), (1,)), ((), ()))
_TN = (((0,), (0,)), ((), ()))


def _params(*sem):
    return pltpu.CompilerParams(dimension_semantics=sem, vmem_limit_bytes=VMEM_LIMIT_BYTES)


def _rms(x, g, eps=NORM_EPS):
    return x * lax.rsqrt(jnp.mean(x * x, axis=-1, keepdims=True) + eps) * g


def _row_chunks(rows, chunks=ROW_CHUNKS):
    size = rows // chunks if rows % (chunks * 16) == 0 else rows
    return [slice(r, r + size) for r in range(0, rows, size)]


def _cast_rows(w_refs, wb_ref):
    rows = w_refs[0].shape[0]
    for s, w_ref in enumerate(w_refs):
        wb_ref[s * rows:(s + 1) * rows, :] = w_ref[...].astype(BF16)


def _norm_matmul_kernel(x_ref, g_ref, *refs):
    w_refs, (o_ref, wb_ref) = refs[:WEIGHT_SPLITS], refs[WEIGHT_SPLITS:]

    @pl.when(pl.program_id(1) == 0)
    def _():
        _cast_rows(w_refs, wb_ref)

    for rows in _row_chunks(x_ref.shape[0]):
        xn = _rms(x_ref[rows, :], g_ref[...]).astype(BF16)
        o_ref[rows, :] = jnp.dot(xn, wb_ref[...], preferred_element_type=F32).astype(o_ref.dtype)


def norm_matmul(x, g, w, out_dtype, tm=1024, tn=2048):
    m, d = x.shape
    n = w.shape[1]
    tm = min(tm, m)
    return pl.pallas_call(
        _norm_matmul_kernel,
        grid=(n // tn, m // tm),
        in_specs=[pl.BlockSpec((tm, d), lambda j, i: (i, 0)),
                  pl.BlockSpec((1, d), lambda j, i: (0, 0))]
                 + [pl.BlockSpec((d // WEIGHT_SPLITS, tn), functools.partial(lambda j, i, s: (s, j), s=s))
                    for s in range(WEIGHT_SPLITS)],
        out_specs=pl.BlockSpec((tm, tn), lambda j, i: (i, j)),
        out_shape=jax.ShapeDtypeStruct((m, n), out_dtype),
        scratch_shapes=[pltpu.VMEM((d, tn), BF16)],
        compiler_params=_params("parallel", "arbitrary"),
        name="norm_matmul",
    )(x, g.reshape(1, d), *([w] * WEIGHT_SPLITS))


def _resident_weight(shape):
    rows, cols = shape
    return [pl.BlockSpec((rows // WEIGHT_SPLITS, cols), functools.partial(lambda i, s: (s, 0), s=s),
                         pipeline_mode=pl.Buffered(1)) for s in range(WEIGHT_SPLITS)]


def _matmul_residual_kernel(a_ref, *refs):
    w_refs, (r_ref, o_ref, wb_ref) = refs[:WEIGHT_SPLITS], refs[WEIGHT_SPLITS:]

    @pl.when(pl.program_id(0) == 0)
    def _():
        _cast_rows(w_refs, wb_ref)

    o_ref[...] = r_ref[...] + jnp.dot(a_ref[...], wb_ref[...], preferred_element_type=F32)


def matmul_residual(a, w, res, tm=1024):
    m, k = a.shape
    n = w.shape[1]
    tm = min(tm, m)
    return pl.pallas_call(
        _matmul_residual_kernel,
        grid=(m // tm,),
        in_specs=[pl.BlockSpec((tm, k), lambda i: (i, 0)),
                  *_resident_weight((k, n)),
                  pl.BlockSpec((tm, n), lambda i: (i, 0))],
        out_specs=pl.BlockSpec((tm, n), lambda i: (i, 0)),
        out_shape=jax.ShapeDtypeStruct((m, n), F32),
        scratch_shapes=[pltpu.VMEM((k, n), BF16)],
        compiler_params=_params("arbitrary"),
        name="matmul_residual",
    )(a, *([w] * WEIGHT_SPLITS), res)


def _mlp_kernel(x_ref, g_ref, wu_ref, wd_ref, gf_ref, o_ref, xn_ref, acc_ref, *, final_norm):
    f = pl.program_id(1)
    last = pl.num_programs(1) - 1

    def weights():
        return wu_ref[...].astype(BF16), wd_ref[...].astype(BF16)

    def hidden(xn, wu, wd):
        u = jnp.maximum(jnp.dot(xn, wu, preferred_element_type=F32), 0.0)
        return jnp.dot((u * u).astype(BF16), wd, preferred_element_type=F32)

    @pl.when(f == 0)
    def _():
        wu, wd = weights()
        for rows in _row_chunks(x_ref.shape[0]):
            xn = _rms(x_ref[rows, :], g_ref[...]).astype(BF16)
            xn_ref[rows, :] = xn
            acc_ref[rows, :] = hidden(xn, wu, wd)

    @pl.when(jnp.logical_and(f != 0, f != last))
    def _():
        acc_ref[...] += hidden(xn_ref[...], *weights())

    @pl.when(f == last)
    def _():
        wu, wd = weights()
        for rows in _row_chunks(x_ref.shape[0]):
            h = x_ref[rows, :] + acc_ref[rows, :] + hidden(xn_ref[rows, :], wu, wd)
            o_ref[rows, :] = _rms(h, gf_ref[...]) if final_norm else h


def mlp(x, g, w_up, w_down, layer, g_final=None, tm=1024, tf=1024):
    m, d = x.shape
    ff = w_up.shape[2]
    assert ff // tf >= 2
    tm = min(tm, m)
    final_norm = g_final is not None
    gf = g_final if final_norm else g
    return pl.pallas_call(
        functools.partial(_mlp_kernel, final_norm=final_norm),
        grid=(m // tm, ff // tf),
        in_specs=[pl.BlockSpec((tm, d), lambda i, f: (i, 0)),
                  pl.BlockSpec((1, d), lambda i, f: (0, 0)),
                  pl.BlockSpec((None, d, tf), lambda i, f: (layer, 0, f)),
                  pl.BlockSpec((None, tf, d), lambda i, f: (layer, f, 0)),
                  pl.BlockSpec((1, d), lambda i, f: (0, 0))],
        out_specs=pl.BlockSpec((tm, d), lambda i, f: (i, 0)),
        out_shape=jax.ShapeDtypeStruct((m, d), F32),
        scratch_shapes=[pltpu.VMEM((tm, d), BF16), pltpu.VMEM((tm, d), F32)],
        compiler_params=_params("parallel", "arbitrary"),
        name="mlp",
    )(x, g.reshape(1, d), w_up, w_down, gf.reshape(1, d))


def _kvq_kernel(x_ref, gkv_ref, gq_ref, *refs):
    wkv_refs, wq_refs = refs[:WEIGHT_SPLITS], refs[WEIGHT_SPLITS:2 * WEIGHT_SPLITS]
    k_ref, v_ref, q_ref, wkvb_ref, wqb_ref = refs[2 * WEIGHT_SPLITS:]

    @pl.when(pl.program_id(0) == 0)
    def _():
        _cast_rows(wkv_refs, wkvb_ref)
        _cast_rows(wq_refs, wqb_ref)

    d = x_ref.shape[1]
    for rows in _row_chunks(x_ref.shape[0], 2):
        x = x_ref[rows, :]
        r = x * lax.rsqrt(jnp.mean(x * x, axis=-1, keepdims=True) + NORM_EPS)
        xkv = (r * gkv_ref[...]).astype(BF16)
        xq = (r * gq_ref[...]).astype(BF16)
        k_ref[rows, :] = jnp.dot(xkv, wkvb_ref[:, :d], preferred_element_type=F32)
        v_ref[rows, :] = jnp.dot(xkv, wkvb_ref[:, d:], preferred_element_type=F32)
        q_ref[rows, :] = jnp.dot(xq, wqb_ref[...], preferred_element_type=F32)


def kvq_proj(x, g_kv, g_q, w_kv, w_q, tm=512):
    m, d = x.shape
    tm = min(tm, m)
    row = pl.BlockSpec((tm, d), lambda i: (i, 0))
    gain = pl.BlockSpec((1, d), lambda i: (0, 0))
    out = jax.ShapeDtypeStruct((m, d), F32)
    return pl.pallas_call(
        _kvq_kernel,
        grid=(m // tm,),
        in_specs=[row, gain, gain, *_resident_weight((d, 2 * d)), *_resident_weight((d, d))],
        out_specs=[row, row, row],
        out_shape=[out, out, out],
        scratch_shapes=[pltpu.VMEM((d, 2 * d), BF16), pltpu.VMEM((d, d), BF16)],
        compiler_params=_params("arbitrary"),
        name="kvq_proj",
    )(x, g_kv.reshape(1, d), g_q.reshape(1, d), *([w_kv] * WEIGHT_SPLITS), *([w_q] * WEIGHT_SPLITS))


def _ret_tables(first_pos, length, chunk):
    angle = 1.0 / (10000.0 ** jnp.linspace(0.0, 1.0, RET_DK // 2, dtype=F32))
    angle = jnp.repeat(angle, 2)
    pos = first_pos + jnp.arange(length, dtype=jnp.int32)
    ang = pos.astype(F32)[:, None] * angle[None, :]
    sin, cos = jnp.sin(ang), jnp.cos(ang)
    even = (jnp.arange(RET_DK) % 2 == 0)[None, :]
    sin_next = jnp.where(even, -sin, 0.0)
    sin_prev = jnp.where(even, 0.0, sin)

    lg = jnp.log(1.0 - 2.0 ** (-5.0 - jnp.arange(RET_HEADS, dtype=F32)))
    i = jnp.arange(chunk, dtype=F32)
    diff = i[:, None] - i[None, :]
    d_intra = jnp.exp(jnp.where(diff >= 0, lg[:, None, None] * diff, -jnp.inf))
    q_dec = jnp.exp(lg[:, None] * (i + 1.0))
    k_dec = jnp.exp(lg[:, None] * (chunk - 1.0 - i))
    c_dec = jnp.exp(lg * chunk)
    d_intra = d_intra * RET_DK ** -0.5
    k_dec = k_dec * RET_DK ** -0.5
    pad = RET_ROWS - chunk
    d_intra = jnp.pad(d_intra, ((0, 0), (0, pad), (0, pad)))
    q_dec = jnp.broadcast_to(jnp.pad(q_dec, ((0, 0), (0, pad)))[:, :, None], (RET_HEADS, RET_ROWS, RET_DK))
    k_dec = jnp.broadcast_to(jnp.pad(k_dec, ((0, 0), (0, pad)))[:, :, None], (RET_HEADS, RET_ROWS, RET_DK))
    c_dec = jnp.broadcast_to(c_dec[:, None, None], (RET_HEADS, 1, RET_DV))
    return cos, sin_next, sin_prev, d_intra, q_dec, k_dec, c_dec


def _theta_shift(x, cos, sin_next, sin_prev):
    dk = x.shape[-1]
    return x * cos + pltpu.roll(x, dk - 1, 1) * sin_next + pltpu.roll(x, 1, 1) * sin_prev


def _pad_rows(x, rows):
    if x.shape[0] == rows:
        return x
    return jnp.concatenate([x, jnp.zeros((rows - x.shape[0], x.shape[1]), x.dtype)], axis=0)


def _retention_kernel(q_ref, k_ref, v_ref, gate_ref, cos_ref, sn_ref, sp_ref, dintra_ref, qdec_ref, kdec_ref,
                      cdec_ref, s0_ref, o_ref, sout_ref, *stream, chunk, n_chunks, n_batch, has_state):
    seeded = n_chunks > 1
    streamed = bool(stream)
    if seeded:
        @pl.when(pl.program_id(1) == 0)
        def _():
            sout_ref[0] = s0_ref[0] if has_state else jnp.zeros(sout_ref.shape[1:], F32)
    if streamed:
        sin_buf, sout_buf, sem_in, sem_out = stream
        b = pl.program_id(0)
        slot = b % 2

        def fetch(bb, sl, h):
            return pltpu.make_async_copy(s0_ref.at[bb, h], sin_buf.at[sl, h], sem_in.at[sl, h])

        def flush(bb, sl, h):
            return pltpu.make_async_copy(sout_buf.at[sl, h], sout_ref.at[bb, h], sem_out.at[sl, h])

        @pl.when(b == 0)
        def _():
            for h in range(RET_HEADS):
                fetch(0, 0, h).start()

        @pl.when(b + 1 < n_batch)
        def _():
            for h in range(RET_HEADS):
                fetch(b + 1, 1 - slot, h).start()

        @pl.when(b >= 2)
        def _():
            for h in range(RET_HEADS):
                flush(b - 2, slot, h).wait()

    cos, sn, sp = cos_ref[...], sn_ref[...], sp_ref[...]

    def scores(h):
        qk_cols = slice(h * RET_DK, (h + 1) * RET_DK)
        q = _theta_shift(q_ref[0, :, qk_cols].astype(F32), cos, sn, sp)
        k = _theta_shift(k_ref[0, :, qk_cols].astype(F32), cos, sn, sp)
        q = _pad_rows(q, RET_ROWS)
        k = _pad_rows(k, RET_ROWS)
        a = lax.dot_general(q.astype(BF16), k.astype(BF16), _NT, preferred_element_type=F32) * dintra_ref[h]
        return q, k, a

    def finish(h, q, k, a):
        v_cols = slice(h * RET_DV, (h + 1) * RET_DV)
        v = _pad_rows(v_ref[0, :, v_cols], RET_ROWS).astype(BF16)
        if streamed:
            fetch(b, slot, h).wait()
            s = sin_buf[slot, h]
        elif seeded:
            s = sout_ref[0, h]
        else:
            s = s0_ref[0, h] if has_state else jnp.zeros((RET_DK, RET_DV), F32)
        o = (jnp.dot(a.astype(BF16), v, preferred_element_type=F32)
             + jnp.dot((q * qdec_ref[h]).astype(BF16), s.astype(BF16), preferred_element_type=F32))
        s_new = s * cdec_ref[h] + lax.dot_general((k * kdec_ref[h]).astype(BF16), v, _TN,
                                                  preferred_element_type=F32)
        if streamed:
            sout_buf[slot, h] = s_new
            flush(b, slot, h).start()
        else:
            sout_ref[0, h] = s_new
        return o[:chunk]

    def gate_out(h, o):
        v_cols = slice(h * RET_DV, (h + 1) * RET_DV)
        o = o * lax.rsqrt(jnp.mean(o * o, axis=-1, keepdims=True) + GN_EPS)
        half = 0.5 * gate_ref[0, :, v_cols].astype(F32)
        o_ref[0, :, v_cols] = (o * (half + half * jnp.tanh(half))).astype(o_ref.dtype)

    stage1 = {0: scores(0)}
    stage2 = {}
    for h in range(RET_HEADS + 1):
        if h + 1 < RET_HEADS:
            stage1[h + 1] = scores(h + 1)
        if h < RET_HEADS:
            stage2[h] = finish(h, *stage1.pop(h))
        if h >= 1:
            gate_out(h - 1, stage2.pop(h - 1))

    if streamed:
        @pl.when(b == n_batch - 1)
        def _():
            for h in range(RET_HEADS):
                if n_batch > 1:
                    flush(b - 1, 1 - slot, h).wait()
                flush(b, slot, h).wait()


def retention(proj, first_pos, s0):
    b, l, _ = proj.shape
    chunk = math.gcd(l, RET_CHUNK)
    n = l // chunk
    has_state = s0 is not None
    if not has_state:
        s0 = jnp.zeros((1, RET_HEADS, RET_DK, RET_DV), F32)
    tables = _ret_tables(first_pos, l, chunk)
    k_col, v_col, gate_col = 1, 2 * RET_QK // RET_V, 2 * RET_QK // RET_V + 1
    pos_tab = pl.BlockSpec((chunk, RET_DK), lambda bi, c: (c, 0))
    head_tab = lambda rows, cols: pl.BlockSpec((RET_HEADS, rows, cols), lambda bi, c: (0, 0, 0))
    state = lambda index_map: pl.BlockSpec((1, RET_HEADS, RET_DK, RET_DV), index_map)
    streamed = has_state and n == 1
    if streamed:
        state_in = state_out = pl.BlockSpec(memory_space=pl.ANY)
        slots = (2, RET_HEADS, RET_DK, RET_DV)
        scratch = [pltpu.VMEM(slots, F32), pltpu.VMEM(slots, F32),
                   pltpu.SemaphoreType.DMA(slots[:2]), pltpu.SemaphoreType.DMA(slots[:2])]
        semantics = ("arbitrary", "arbitrary")
    else:
        state_in = state((lambda bi, c: (bi, 0, 0, 0)) if has_state else (lambda bi, c: (0, 0, 0, 0)))
        state_out = state(lambda bi, c: (bi, 0, 0, 0))
        scratch = []
        semantics = ("parallel", "arbitrary")
    return pl.pallas_call(
        functools.partial(_retention_kernel, chunk=chunk, n_chunks=n, n_batch=b, has_state=has_state),
        grid=(b, n),
        in_specs=[pl.BlockSpec((1, chunk, RET_QK), lambda bi, c: (bi, c, 0)),
                  pl.BlockSpec((1, chunk, RET_QK), lambda bi, c: (bi, c, k_col)),
                  pl.BlockSpec((1, chunk, RET_V), lambda bi, c: (bi, c, v_col)),
                  pl.BlockSpec((1, chunk, RET_V), lambda bi, c: (bi, c, gate_col)),
                  pos_tab, pos_tab, pos_tab,
                  head_tab(RET_ROWS, RET_ROWS), head_tab(RET_ROWS, RET_DK), head_tab(RET_ROWS, RET_DK),
                  head_tab(1, RET_DV), state_in],
        out_specs=[pl.BlockSpec((1, chunk, RET_V), lambda bi, c: (bi, c, 0)), state_out],
        out_shape=[jax.ShapeDtypeStruct((b, l, RET_V), proj.dtype),
                   jax.ShapeDtypeStruct((b, RET_HEADS, RET_DK, RET_DV), F32)],
        scratch_shapes=scratch,
        compiler_params=_params(*semantics),
        name="retention",
    )(proj, proj, proj, proj, *tables, s0)


def _moba_prompt_kernel(q_ref, k_ref, v_ref, o_ref, *, nb, tile_blocks):
    k = k_ref[0]
    s_len = k.shape[0]
    means = jnp.sum(k.reshape(nb, MOBA_BLOCK, ATTN_DH), axis=1) * (1.0 / MOBA_BLOCK)
    means = _pad_rows(means, -(-nb // 8) * 8)
    key_lane = lax.broadcasted_iota(jnp.int32, (s_len, 128), 1)
    key_block = lax.broadcasted_iota(jnp.int32, (s_len, 128), 0) // MOBA_BLOCK
    kb = jnp.concatenate([k.astype(BF16), jnp.where(key_block == key_lane, 1.0, 0.0).astype(BF16)], axis=1)
    vb = jnp.concatenate([v_ref[0].astype(BF16), jnp.where(key_lane == 0, 1.0, 0.0).astype(BF16)], axis=1)
    tile = tile_blocks * MOBA_BLOCK
    n_tiles = nb // tile_blocks
    blk = lax.broadcasted_iota(jnp.int32, (means.shape[0], tile), 0)
    own_blk = lax.broadcasted_iota(jnp.int32, (means.shape[0], tile), 1) // MOBA_BLOCK
    row = lax.broadcasted_iota(jnp.int32, (tile, tile), 0)
    col = lax.broadcasted_iota(jnp.int32, (tile, tile), 1)
    causal = col <= row
    eye = jnp.where(lax.broadcasted_iota(jnp.int32, (MOBA_BLOCK, MOBA_BLOCK), 0)
                    == lax.broadcasted_iota(jnp.int32, (MOBA_BLOCK, MOBA_BLOCK), 1), 1.0, 0.0).astype(BF16)

    def masked_logits(i):
        q = q_ref[0, i * tile:(i + 1) * tile, :]
        first = i * tile_blocks
        bias = jnp.zeros((tile, 128), BF16)
        if first + tile_blocks > 1:
            s = lax.dot_general(means, q, _NT, preferred_element_type=F32, precision=lax.Precision.HIGHEST)
            elig = blk < own_blk + first
            sm = jnp.where(elig, s, NEG)
            rank = jnp.zeros(sm.shape, F32)
            for m in range(first + tile_blocks - 1):
                cm = sm[m:m + 1, :]
                beats = jnp.where(cm > sm, 1.0, jnp.where(cm == sm, jnp.where(blk > m, 1.0, 0.0), 0.0))
                rank = rank + beats
            keep = jnp.where(rank < MOBA_TOPK, jnp.where(sm > 0.5 * NEG, 0.0, NEG), NEG)
            bias_t = _pad_rows(jnp.where(elig, keep, 0.0), 128)
            bias = jnp.concatenate([
                lax.dot_general(eye, bias_t[:, j * MOBA_BLOCK:(j + 1) * MOBA_BLOCK].astype(BF16), _NT,
                                preferred_element_type=F32) for j in range(tile_blocks)], axis=0).astype(BF16)
        qa = jnp.concatenate([(q * (ATTN_DH ** -0.5 * LOG2_E)).astype(BF16), bias], axis=1)
        kv_rows = (i + 1) * tile
        logits = lax.dot_general(qa, kb[:kv_rows], _NT, preferred_element_type=F32)
        own = jnp.where(causal, logits[:, i * tile:], NEG)
        return own if i == 0 else jnp.concatenate([logits[:, :i * tile], own], axis=1)

    def attend(i, logits):
        p = jnp.exp2(logits - jnp.max(logits, axis=-1, keepdims=True))
        acc = jnp.dot(p.astype(BF16), vb[:(i + 1) * tile], preferred_element_type=F32)
        o_ref[0, i * tile:(i + 1) * tile, :] = (
            acc[:, :ATTN_DH] / acc[:, ATTN_DH:ATTN_DH + 1]).astype(o_ref.dtype)

    order = list(range(n_tiles - 1, -1, -1))
    logits = masked_logits(order[0])
    for n, i in enumerate(order):
        following = masked_logits(order[n + 1]) if n + 1 < n_tiles else None
        attend(i, logits)
        logits = following


def moba_prompt(q, k, v):
    b, s, _ = q.shape
    assert s % (MOBA_BLOCK * PROMPT_TILE_BLOCKS) == 0
    nb = s // MOBA_BLOCK
    spec = pl.BlockSpec((1, s, ATTN_DH), lambda bi, h: (bi, 0, h))
    return pl.pallas_call(
        functools.partial(_moba_prompt_kernel, nb=nb, tile_blocks=PROMPT_TILE_BLOCKS),
        grid=(b, ATTN_HEADS),
        in_specs=[spec, spec, spec],
        out_specs=spec,
        out_shape=jax.ShapeDtypeStruct((b, s, ATTN_HEADS * ATTN_DH), BF16),
        compiler_params=_params("parallel", "parallel"),
        name="moba_prompt",
    )(q, k, v)


def _block_diag_queries(q):
    l, d = q.shape
    tiled = jnp.concatenate([q] * ATTN_HEADS, axis=0)
    row_head = lax.broadcasted_iota(jnp.int32, tiled.shape, 0) // l
    col_head = lax.broadcasted_iota(jnp.int32, tiled.shape, 1) // ATTN_DH
    return jnp.where(row_head == col_head, tiled, 0.0)


def _page_rows(page_ref):
    heads = [page_ref[pl.ds(h, PAGE_SIZE, stride=ATTN_HEADS), :] for h in range(ATTN_HEADS)]
    return jnp.concatenate(heads, axis=1).astype(BF16)


class _PageStream:
    def __init__(self, pt_ref, cache_refs, buf_ref, sem_ref, n_pages):
        assert n_pages % STEP_PAGES == 0
        self.pt_ref, self.cache_refs, self.buf_ref, self.sem_ref = pt_ref, cache_refs, buf_ref, sem_ref
        self.n_pages = n_pages
        self.span = len(cache_refs) * n_pages
        self.total = pl.num_programs(0) * self.span
        self.base = pl.program_id(0) * self.span

    def _copy(self, cache_ref, page, g):
        slot = g % RING_PAGES
        return pltpu.make_async_copy(cache_ref.at[page], self.buf_ref.at[slot], self.sem_ref.at[slot])

    def _start_group(self, g0):
        seq, within = g0 // self.span, g0 % self.span
        which, first = within // self.n_pages, within % self.n_pages
        for c, cache_ref in enumerate(self.cache_refs):
            @pl.when(which == c)
            def _(cache_ref=cache_ref):
                for j in range(STEP_PAGES):
                    self._copy(cache_ref, self.pt_ref[seq, first + j], g0 + j).start()

    def prime(self):
        @pl.when(pl.program_id(0) == 0)
        def _():
            for g0 in range(0, RING_PAGES - STEP_PAGES, STEP_PAGES):
                self._start_group(jnp.int32(g0))

    def group(self, it):
        g0 = self.base + it * STEP_PAGES
        ahead = g0 + (RING_PAGES - STEP_PAGES)

        @pl.when(ahead < self.total)
        def _():
            self._start_group(ahead)

        pages = []
        for j in range(STEP_PAGES):
            self._copy(self.cache_refs[0], 0, g0 + j).wait()
            pages.append(self.buf_ref.at[(g0 + j) % RING_PAGES])
        return pages


def _moba_sample_kernel(pt_ref, q_ref, kn_ref, vn_ref, ck_ref, cv_ref, o_ref,
                        buf_ref, sem_ref, lg_ref, bsum_ref, sel_ref, m_ref, l_ref, acc_ref,
                        *, n_new, n_past_blocks, n_pages):
    stream = _PageStream(pt_ref, (ck_ref, cv_ref), buf_ref, sem_ref, n_pages)
    stream.prime()
    groups = n_pages // STEP_PAGES
    wq = _block_diag_queries(q_ref[0])
    w = (wq * ATTN_DH ** -0.5).astype(BF16)
    rows = wq.shape[0]
    lane = lax.broadcasted_iota(jnp.int32, (rows, 128), 1)

    def keys_body(it, carry):
        pages = stream.group(it)
        sums = [jnp.sum(p[...].reshape(PAGE_SIZE, ATTN_HEADS, ATTN_DH), axis=0) for p in pages]
        keys = jnp.concatenate([_page_rows(p) for p in pages], axis=0)
        logits = lax.dot_general(w, keys, _NT, preferred_element_type=F32)
        for j in range(STEP_BLOCKS):
            blk = it * STEP_BLOCKS + j
            bsum_ref[pl.ds(pl.multiple_of(blk * ATTN_HEADS, ATTN_HEADS), ATTN_HEADS), :] = functools.reduce(
                lambda x, y: x + y, sums[j * PAGES_PER_BLOCK:(j + 1) * PAGES_PER_BLOCK])
            lg_ref[blk] = logits[:, j * MOBA_BLOCK:(j + 1) * MOBA_BLOCK]
        return carry

    lax.fori_loop(0, groups, keys_body, 0)

    kn = kn_ref[0]
    past_sums = jnp.concatenate(
        [bsum_ref[pl.ds(h, n_past_blocks, stride=ATTN_HEADS), :] for h in range(ATTN_HEADS)], axis=1)
    first_row = lax.broadcasted_iota(jnp.int32, (8, kn.shape[1]), 0) == 0
    own_sum = jnp.where(first_row, jnp.sum(kn, axis=0, keepdims=True), 0.0)
    means = _pad_rows(jnp.concatenate([past_sums, own_sum], axis=0) * (1.0 / MOBA_BLOCK), 128)
    s = lax.dot_general(wq, means, _NT, preferred_element_type=F32, precision=lax.Precision.HIGHEST)
    work = jnp.where(lane < n_past_blocks, s, NEG)
    sel = jnp.zeros(work.shape, F32)
    for _ in range(MOBA_TOPK):
        mx = jnp.max(work, axis=-1, keepdims=True)
        first = jnp.min(jnp.where(work == mx, lane, 128), axis=-1, keepdims=True)
        pick = lane == first
        sel = jnp.where(pick, jnp.where(mx > 0.5 * NEG, 1.0, 0.0), sel)
        work = jnp.where(pick, -jnp.inf, work)
    sel_ref[...] = sel
    t = lax.broadcasted_iota(jnp.int32, (rows, 128), 0) % n_new
    lo = lax.dot_general(w, _pad_rows(kn, 128).astype(BF16), _NT, preferred_element_type=F32)
    lo = jnp.where(lane <= t, lo, NEG)
    m0 = jnp.max(lo, axis=-1, keepdims=True)
    p0 = jnp.exp(lo - m0)
    m_ref[...] = jnp.broadcast_to(m0, m_ref.shape)
    l_ref[...] = jnp.broadcast_to(jnp.sum(p0, axis=-1, keepdims=True), l_ref.shape)
    acc_ref[...] = jnp.dot(p0.astype(BF16), _pad_rows(vn_ref[0], 128).astype(BF16), preferred_element_type=F32)

    def values_body(it, carry):
        pages = stream.group(groups + it)
        sel = sel_ref[...]
        lgs = []
        for j in range(STEP_BLOCKS):
            blk = it * STEP_BLOCKS + j
            picked = jnp.max(jnp.where(lane == blk, sel, 0.0), axis=-1, keepdims=True) > 0.5
            lgs.append(jnp.where(picked, lg_ref[blk], NEG))
        m_old = m_ref[:, :1]
        m_new = functools.reduce(jnp.maximum, [m_old] + [jnp.max(lg, axis=-1, keepdims=True) for lg in lgs])
        alpha = jnp.exp(m_old - m_new)
        ps = [jnp.exp(lg - m_new) for lg in lgs]
        m_ref[...] = jnp.broadcast_to(m_new, m_ref.shape)
        l_ref[...] = alpha * l_ref[...] + functools.reduce(
            lambda x, y: x + y, [jnp.sum(p, axis=-1, keepdims=True) for p in ps])
        values = jnp.concatenate([_page_rows(p) for p in pages], axis=0)
        acc_ref[...] = alpha * acc_ref[...] + jnp.dot(
            jnp.concatenate(ps, axis=1).astype(BF16), values, preferred_element_type=F32)
        return carry

    lax.fori_loop(0, groups, values_body, 0)

    out = acc_ref[...] / l_ref[:, :1]
    for h in range(ATTN_HEADS):
        o_ref[0, :, h * ATTN_DH:(h + 1) * ATTN_DH] = (
            out[h * n_new:(h + 1) * n_new, h * ATTN_DH:(h + 1) * ATTN_DH].astype(o_ref.dtype))


def moba_sample(q, k_new, v_new, cache_k, cache_v, page_table):
    b, l, d = q.shape
    n_pages = PAST_LEN // PAGE_SIZE
    assert n_pages % STEP_PAGES == 0 and (PAST_LEN + l - 1) // MOBA_BLOCK == PAST_LEN // MOBA_BLOCK
    nblk = n_pages // PAGES_PER_BLOCK
    rows = ATTN_HEADS * l
    page_rows = PAGE_SIZE * ATTN_HEADS
    ck = cache_k.reshape(-1, page_rows, ATTN_DH)
    cv = cache_v.reshape(-1, page_rows, ATTN_DH)

    per_batch = lambda *shape: pl.BlockSpec((1,) + shape, lambda bi, pt: (bi,) + (0,) * len(shape))
    cache = pl.BlockSpec(memory_space=pl.ANY)
    return pl.pallas_call(
        functools.partial(_moba_sample_kernel, n_new=l, n_past_blocks=nblk, n_pages=n_pages),
        grid_spec=pltpu.PrefetchScalarGridSpec(
            num_scalar_prefetch=1,
            grid=(b,),
            in_specs=[per_batch(l, d), per_batch(l, d), per_batch(l, d), cache, cache],
            out_specs=per_batch(l, d),
            scratch_shapes=[pltpu.VMEM((RING_PAGES, page_rows, ATTN_DH), F32),
                            pltpu.SemaphoreType.DMA((RING_PAGES,)),
                            pltpu.VMEM((nblk, rows, MOBA_BLOCK), F32),
                            pltpu.VMEM((nblk * ATTN_HEADS, ATTN_DH), F32),
                            pltpu.VMEM((rows, 128), F32),
                            pltpu.VMEM((rows, 128), F32),
                            pltpu.VMEM((rows, 128), F32),
                            pltpu.VMEM((rows, d), F32)]),
        out_shape=jax.ShapeDtypeStruct((b, l, d), F32),
        compiler_params=_params("arbitrary"),
        name="moba_sample",
    )(page_table, q, k_new, v_new, ck, cv)


def kernel(x_prompt, x_sample, state_ret, cache_k, cache_v, page_table, ret_norm_g, ret_w_in, ret_w_out,
           attn_norm_g, attn_w_q, attn_w_out, kv_norm_g, w_kv, mlp_norm_g, mlp_w_up, mlp_w_down, final_norm_g):
    bp, lp, d = x_prompt.shape
    bs, ls, _ = x_sample.shape
    assert ret_w_in.shape[0] == 1 and attn_w_q.shape[0] == 1 and mlp_w_up.shape[0] == 2

    def trunk(x, pos, s0, attend):
        b, l, _ = x.shape
        h = x.reshape(b * l, d)
        act = BF16 if l % RET_ROWS == 0 else F32
        proj = norm_matmul(h, ret_norm_g[0], ret_w_in[0], act)
        o, s = retention(proj.reshape(b, l, -1), pos, s0)
        h = matmul_residual(o.reshape(b * l, RET_V).astype(BF16), ret_w_out[0], h)
        h = mlp(h, mlp_norm_g[0], mlp_w_up, mlp_w_down, 0)
        k, v, q = kvq_proj(h, kv_norm_g, attn_norm_g[0], w_kv, attn_w_q[0])
        o = attend(q.reshape(b, l, d), k.reshape(b, l, d), v.reshape(b, l, d))
        h = matmul_residual(o.reshape(b * l, d).astype(BF16), attn_w_out[0], h)
        y = mlp(h, mlp_norm_g[1], mlp_w_up, mlp_w_down, 1, g_final=final_norm_g)
        kv_shape = (b, l, ATTN_HEADS, ATTN_DH)
        return y.reshape(b, l, d), s[None], k.reshape(kv_shape), v.reshape(kv_shape)

    y_p, s_p, k_p, v_p = trunk(x_prompt, 0, None, moba_prompt)
    y_s, s_s, k_s, v_s = trunk(
        x_sample, PAST_LEN, state_ret[0],
        lambda q, k, v: moba_sample(q, k, v, cache_k, cache_v, page_table))
    return (y_p, y_s, s_p, s_s, k_p, v_p, k_s, v_s)
```

```python
import functools
import math

import jax
import jax.numpy as jnp
from jax import lax
from jax.experimental import pallas as pl
from jax.experimental.pallas import tpu as pltpu

F32 = jnp.float32
BF16 = jnp.bfloat16

D_MODEL = 1024
PAST_LEN = 8192
PAGE_SIZE = 128
RET_HEADS = 4
RET_DK = D_MODEL // RET_HEADS
RET_QK = RET_HEADS * RET_DK
RET_V = 2 * D_MODEL
RET_DV = RET_V // RET_HEADS
RET_CHUNK = 128
ATTN_HEADS = 8
ATTN_DH = D_MODEL // ATTN_HEADS
MOBA_BLOCK = 256
MOBA_TOPK = 3
D_FF = 4 * D_MODEL
NORM_EPS = 1e-5
GN_EPS = 1e-6
NEG = -1e30
LOG2_E = math.log2(math.e)

PAGES_PER_BLOCK = MOBA_BLOCK // PAGE_SIZE
STEP_PAGES = 8
STEP_BLOCKS = STEP_PAGES // PAGES_PER_BLOCK
RING_PAGES = 3 * STEP_PAGES
PROMPT_TILE_BLOCKS = 2
ROW_CHUNKS = 4
WEIGHT_SPLITS = 4
RET_ROWS = 128
VMEM_LIMIT_BYTES = 56 * 1024 * 1024

_NT = (((1,), (1,)), ((), ()))
_TN = (((0,), (0,)), ((), ()))


def _params(*sem):
    return pltpu.CompilerParams(dimension_semantics=sem, vmem_limit_bytes=VMEM_LIMIT_BYTES)


def _rms(x, g, eps=NORM_EPS):
    return x * lax.rsqrt(jnp.mean(x * x, axis=-1, keepdims=True) + eps) * g


def _row_chunks(rows, chunks=ROW_CHUNKS):
    size = rows // chunks if rows % (chunks * 16) == 0 else rows
    return [slice(r, r + size) for r in range(0, rows, size)]


def _cast_rows(w_refs, wb_ref):
    rows = w_refs[0].shape[0]
    for s, w_ref in enumerate(w_refs):
        wb_ref[s * rows:(s + 1) * rows, :] = w_ref[...].astype(BF16)


def _norm_matmul_kernel(x_ref, g_ref, *refs):
    w_refs, (o_ref, wb_ref) = refs[:WEIGHT_SPLITS], refs[WEIGHT_SPLITS:]

    @pl.when(pl.program_id(1) == 0)
    def _():
        _cast_rows(w_refs, wb_ref)

    for rows in _row_chunks(x_ref.shape[0]):
        xn = _rms(x_ref[rows, :], g_ref[...]).astype(BF16)
        o_ref[rows, :] = jnp.dot(xn, wb_ref[...], preferred_element_type=F32).astype(o_ref.dtype)


def norm_matmul(x, g, w, out_dtype, tm=1024, tn=2048):
    m, d = x.shape
    n = w.shape[1]
    tm = min(tm, m)
    return pl.pallas_call(
        _norm_matmul_kernel,
        grid=(n // tn, m // tm),
        in_specs=[pl.BlockSpec((tm, d), lambda j, i: (i, 0)),
                  pl.BlockSpec((1, d), lambda j, i: (0, 0))]
                 + [pl.BlockSpec((d // WEIGHT_SPLITS, tn), functools.partial(lambda j, i, s: (s, j), s=s))
                    for s in range(WEIGHT_SPLITS)],
        out_specs=pl.BlockSpec((tm, tn), lambda j, i: (i, j)),
        out_shape=jax.ShapeDtypeStruct((m, n), out_dtype),
        scratch_shapes=[pltpu.VMEM((d, tn), BF16)],
        compiler_params=_params("parallel", "arbitrary"),
        name="norm_matmul",
    )(x, g.reshape(1, d), *([w] * WEIGHT_SPLITS))


def _resident_weight(shape):
    rows, cols = shape
    return [pl.BlockSpec((rows // WEIGHT_SPLITS, cols), functools.partial(lambda i, s: (s, 0), s=s),
                         pipeline_mode=pl.Buffered(1)) for s in range(WEIGHT_SPLITS)]


def _matmul_residual_kernel(a_ref, *refs):
    w_refs, (r_ref, o_ref, wb_ref) = refs[:WEIGHT_SPLITS], refs[WEIGHT_SPLITS:]

    @pl.when(pl.program_id(0) == 0)
    def _():
        _cast_rows(w_refs, wb_ref)

    o_ref[...] = r_ref[...] + jnp.dot(a_ref[...], wb_ref[...], preferred_element_type=F32)


def matmul_residual(a, w, res, tm=1024):
    m, k = a.shape
    n = w.shape[1]
    tm = min(tm, m)
    return pl.pallas_call(
        _matmul_residual_kernel,
        grid=(m // tm,),
        in_specs=[pl.BlockSpec((tm, k), lambda i: (i, 0)),
                  *_resident_weight((k, n)),
                  pl.BlockSpec((tm, n), lambda i: (i, 0))],
        out_specs=pl.BlockSpec((tm, n), lambda i: (i, 0)),
        out_shape=jax.ShapeDtypeStruct((m, n), F32),
        scratch_shapes=[pltpu.VMEM((k, n), BF16)],
        compiler_params=_params("arbitrary"),
        name="matmul_residual",
    )(a, *([w] * WEIGHT_SPLITS), res)


def _mlp_kernel(x_ref, g_ref, wu_ref, wd_ref, gf_ref, o_ref, xn_ref, acc_ref, *, final_norm):
    f = pl.program_id(1)
    last = pl.num_programs(1) - 1

    def weights():
        return wu_ref[...].astype(BF16), wd_ref[...].astype(BF16)

    def hidden(xn, wu, wd):
        u = jnp.maximum(jnp.dot(xn, wu, preferred_element_type=F32), 0.0)
        return jnp.dot((u * u).astype(BF16), wd, preferred_element_type=F32)

    @pl.when(f == 0)
    def _():
        wu, wd = weights()
        for rows in _row_chunks(x_ref.shape[0]):
            xn = _rms(x_ref[rows, :], g_ref[...]).astype(BF16)
            xn_ref[rows, :] = xn
            acc_ref[rows, :] = hidden(xn, wu, wd)

    @pl.when(jnp.logical_and(f != 0, f != last))
    def _():
        acc_ref[...] += hidden(xn_ref[...], *weights())

    @pl.when(f == last)
    def _():
        wu, wd = weights()
        for rows in _row_chunks(x_ref.shape[0]):
            h = x_ref[rows, :] + acc_ref[rows, :] + hidden(xn_ref[rows, :], wu, wd)
            o_ref[rows, :] = _rms(h, gf_ref[...]) if final_norm else h


def mlp(x, g, w_up, w_down, layer, g_final=None, tm=1024, tf=1024):
    m, d = x.shape
    ff = w_up.shape[2]
    assert ff // tf >= 2
    tm = min(tm, m)
    final_norm = g_final is not None
    gf = g_final if final_norm else g
    return pl.pallas_call(
        functools.partial(_mlp_kernel, final_norm=final_norm),
        grid=(m // tm, ff // tf),
        in_specs=[pl.BlockSpec((tm, d), lambda i, f: (i, 0)),
                  pl.BlockSpec((1, d), lambda i, f: (0, 0)),
                  pl.BlockSpec((None, d, tf), lambda i, f: (layer, 0, f)),
                  pl.BlockSpec((None, tf, d), lambda i, f: (layer, f, 0)),
                  pl.BlockSpec((1, d), lambda i, f: (0, 0))],
        out_specs=pl.BlockSpec((tm, d), lambda i, f: (i, 0)),
        out_shape=jax.ShapeDtypeStruct((m, d), F32),
        scratch_shapes=[pltpu.VMEM((tm, d), BF16), pltpu.VMEM((tm, d), F32)],
        compiler_params=_params("parallel", "arbitrary"),
        name="mlp",
    )(x, g.reshape(1, d), w_up, w_down, gf.reshape(1, d))


def _kvq_kernel(x_ref, gkv_ref, gq_ref, *refs):
    wkv_refs, wq_refs = refs[:WEIGHT_SPLITS], refs[WEIGHT_SPLITS:2 * WEIGHT_SPLITS]
    k_ref, v_ref, q_ref, wkvb_ref, wqb_ref = refs[2 * WEIGHT_SPLITS:]

    @pl.when(pl.program_id(0) == 0)
    def _():
        _cast_rows(wkv_refs, wkvb_ref)
        _cast_rows(wq_refs, wqb_ref)

    d = x_ref.shape[1]
    for rows in _row_chunks(x_ref.shape[0], 2):
        x = x_ref[rows, :]
        r = x * lax.rsqrt(jnp.mean(x * x, axis=-1, keepdims=True) + NORM_EPS)
        xkv = (r * gkv_ref[...]).astype(BF16)
        xq = (r * gq_ref[...]).astype(BF16)
        k_ref[rows, :] = jnp.dot(xkv, wkvb_ref[:, :d], preferred_element_type=F32)
        v_ref[rows, :] = jnp.dot(xkv, wkvb_ref[:, d:], preferred_element_type=F32)
        q_ref[rows, :] = jnp.dot(xq, wqb_ref[...], preferred_element_type=F32)


def kvq_proj(x, g_kv, g_q, w_kv, w_q, tm=512):
    m, d = x.shape
    tm = min(tm, m)
    row = pl.BlockSpec((tm, d), lambda i: (i, 0))
    gain = pl.BlockSpec((1, d), lambda i: (0, 0))
    out = jax.ShapeDtypeStruct((m, d), F32)
    return pl.pallas_call(
        _kvq_kernel,
        grid=(m // tm,),
        in_specs=[row, gain, gain, *_resident_weight((d, 2 * d)), *_resident_weight((d, d))],
        out_specs=[row, row, row],
        out_shape=[out, out, out],
        scratch_shapes=[pltpu.VMEM((d, 2 * d), BF16), pltpu.VMEM((d, d), BF16)],
        compiler_params=_params("arbitrary"),
        name="kvq_proj",
    )(x, g_kv.reshape(1, d), g_q.reshape(1, d), *([w_kv] * WEIGHT_SPLITS), *([w_q] * WEIGHT_SPLITS))


def _ret_tables(first_pos, length, chunk):
    angle = 1.0 / (10000.0 ** jnp.linspace(0.0, 1.0, RET_DK // 2, dtype=F32))
    angle = jnp.repeat(angle, 2)
    pos = first_pos + jnp.arange(length, dtype=jnp.int32)
    ang = pos.astype(F32)[:, None] * angle[None, :]
    sin, cos = jnp.sin(ang), jnp.cos(ang)
    even = (jnp.arange(RET_DK) % 2 == 0)[None, :]
    sin_next = jnp.where(even, -sin, 0.0)
    sin_prev = jnp.where(even, 0.0, sin)

    lg = jnp.log(1.0 - 2.0 ** (-5.0 - jnp.arange(RET_HEADS, dtype=F32)))
    i = jnp.arange(chunk, dtype=F32)
    diff = i[:, None] - i[None, :]
    d_intra = jnp.exp(jnp.where(diff >= 0, lg[:, None, None] * diff, -jnp.inf))
    q_dec = jnp.exp(lg[:, None] * (i + 1.0))
    k_dec = jnp.exp(lg[:, None] * (chunk - 1.0 - i))
    c_dec = jnp.exp(lg * chunk)
    d_intra = d_intra * RET_DK ** -0.5
    k_dec = k_dec * RET_DK ** -0.5
    pad = RET_ROWS - chunk
    d_intra = jnp.pad(d_intra, ((0, 0), (0, pad), (0, pad)))
    q_dec = jnp.broadcast_to(jnp.pad(q_dec, ((0, 0), (0, pad)))[:, :, None], (RET_HEADS, RET_ROWS, RET_DK))
    k_dec = jnp.broadcast_to(jnp.pad(k_dec, ((0, 0), (0, pad)))[:, :, None], (RET_HEADS, RET_ROWS, RET_DK))
    c_dec = jnp.broadcast_to(c_dec[:, None, None], (RET_HEADS, 1, RET_DV))
    return cos, sin_next, sin_prev, d_intra, q_dec, k_dec, c_dec


def _theta_shift(x, cos, sin_next, sin_prev):
    dk = x.shape[-1]
    return x * cos + pltpu.roll(x, dk - 1, 1) * sin_next + pltpu.roll(x, 1, 1) * sin_prev


def _pad_rows(x, rows):
    if x.shape[0] == rows:
        return x
    return jnp.concatenate([x, jnp.zeros((rows - x.shape[0], x.shape[1]), x.dtype)], axis=0)


def _retention_kernel(q_ref, k_ref, v_ref, gate_ref, cos_ref, sn_ref, sp_ref, dintra_ref, qdec_ref, kdec_ref,
                      cdec_ref, s0_ref, o_ref, sout_ref, *stream, chunk, n_chunks, n_batch, has_state):
    seeded = n_chunks > 1
    streamed = bool(stream)
    if seeded:
        @pl.when(pl.program_id(1) == 0)
        def _():
            sout_ref[0] = s0_ref[0] if has_state else jnp.zeros(sout_ref.shape[1:], F32)
    if streamed:
        sin_buf, sout_buf, sem_in, sem_out = stream
        b = pl.program_id(0)
        slot = b % 2

        def fetch(bb, sl, h):
            return pltpu.make_async_copy(s0_ref.at[bb, h], sin_buf.at[sl, h], sem_in.at[sl, h])

        def flush(bb, sl, h):
            return pltpu.make_async_copy(sout_buf.at[sl, h], sout_ref.at[bb, h], sem_out.at[sl, h])

        @pl.when(b == 0)
        def _():
            for h in range(RET_HEADS):
                fetch(0, 0, h).start()

        @pl.when(b + 1 < n_batch)
        def _():
            for h in range(RET_HEADS):
                fetch(b + 1, 1 - slot, h).start()

        @pl.when(b >= 2)
        def _():
            for h in range(RET_HEADS):
                flush(b - 2, slot, h).wait()

    cos, sn, sp = cos_ref[...], sn_ref[...], sp_ref[...]

    def scores(h):
        qk_cols = slice(h * RET_DK, (h + 1) * RET_DK)
        q = _theta_shift(q_ref[0, :, qk_cols].astype(F32), cos, sn, sp)
        k = _theta_shift(k_ref[0, :, qk_cols].astype(F32), cos, sn, sp)
        q = _pad_rows(q, RET_ROWS)
        k = _pad_rows(k, RET_ROWS)
        a = lax.dot_general(q.astype(BF16), k.astype(BF16), _NT, preferred_element_type=F32) * dintra_ref[h]
        return q, k, a

    def finish(h, q, k, a):
        v_cols = slice(h * RET_DV, (h + 1) * RET_DV)
        v = _pad_rows(v_ref[0, :, v_cols], RET_ROWS).astype(BF16)
        if streamed:
            fetch(b, slot, h).wait()
            s = sin_buf[slot, h]
        elif seeded:
            s = sout_ref[0, h]
        else:
            s = s0_ref[0, h] if has_state else jnp.zeros((RET_DK, RET_DV), F32)
        o = (jnp.dot(a.astype(BF16), v, preferred_element_type=F32)
             + jnp.dot((q * qdec_ref[h]).astype(BF16), s.astype(BF16), preferred_element_type=F32))
        s_new = s * cdec_ref[h] + lax.dot_general((k * kdec_ref[h]).astype(BF16), v, _TN,
                                                  preferred_element_type=F32)
        if streamed:
            sout_buf[slot, h] = s_new
            flush(b, slot, h).start()
        else:
            sout_ref[0, h] = s_new
        return o[:chunk]

    def gate_out(h, o):
        v_cols = slice(h * RET_DV, (h + 1) * RET_DV)
        o = o * lax.rsqrt(jnp.mean(o * o, axis=-1, keepdims=True) + GN_EPS)
        gate = gate_ref[0, :, v_cols].astype(F32)
        o_ref[0, :, v_cols] = (o * (gate * jax.nn.sigmoid(gate))).astype(o_ref.dtype)

    stage1 = {0: scores(0)}
    stage2 = {}
    for h in range(RET_HEADS + 1):
        if h + 1 < RET_HEADS:
            stage1[h + 1] = scores(h + 1)
        if h < RET_HEADS:
            stage2[h] = finish(h, *stage1.pop(h))
        if h >= 1:
            gate_out(h - 1, stage2.pop(h - 1))

    if streamed:
        @pl.when(b == n_batch - 1)
        def _():
            for h in range(RET_HEADS):
                if n_batch > 1:
                    flush(b - 1, 1 - slot, h).wait()
                flush(b, slot, h).wait()


def retention(proj, first_pos, s0):
    b, l, _ = proj.shape
    chunk = math.gcd(l, RET_CHUNK)
    n = l // chunk
    has_state = s0 is not None
    if not has_state:
        s0 = jnp.zeros((1, RET_HEADS, RET_DK, RET_DV), F32)
    tables = _ret_tables(first_pos, l, chunk)
    k_col, v_col, gate_col = 1, 2 * RET_QK // RET_V, 2 * RET_QK // RET_V + 1
    pos_tab = pl.BlockSpec((chunk, RET_DK), lambda bi, c: (c, 0))
    head_tab = lambda rows, cols: pl.BlockSpec((RET_HEADS, rows, cols), lambda bi, c: (0, 0, 0))
    state = lambda index_map: pl.BlockSpec((1, RET_HEADS, RET_DK, RET_DV), index_map)
    streamed = has_state and n == 1
    if streamed:
        state_in = state_out = pl.BlockSpec(memory_space=pl.ANY)
        slots = (2, RET_HEADS, RET_DK, RET_DV)
        scratch = [pltpu.VMEM(slots, F32), pltpu.VMEM(slots, F32),
                   pltpu.SemaphoreType.DMA(slots[:2]), pltpu.SemaphoreType.DMA(slots[:2])]
        semantics = ("arbitrary", "arbitrary")
    else:
        state_in = state((lambda bi, c: (bi, 0, 0, 0)) if has_state else (lambda bi, c: (0, 0, 0, 0)))
        state_out = state(lambda bi, c: (bi, 0, 0, 0))
        scratch = []
        semantics = ("parallel", "arbitrary")
    return pl.pallas_call(
        functools.partial(_retention_kernel, chunk=chunk, n_chunks=n, n_batch=b, has_state=has_state),
        grid=(b, n),
        in_specs=[pl.BlockSpec((1, chunk, RET_QK), lambda bi, c: (bi, c, 0)),
                  pl.BlockSpec((1, chunk, RET_QK), lambda bi, c: (bi, c, k_col)),
                  pl.BlockSpec((1, chunk, RET_V), lambda bi, c: (bi, c, v_col)),
                  pl.BlockSpec((1, chunk, RET_V), lambda bi, c: (bi, c, gate_col)),
                  pos_tab, pos_tab, pos_tab,
                  head_tab(RET_ROWS, RET_ROWS), head_tab(RET_ROWS, RET_DK), head_tab(RET_ROWS, RET_DK),
                  head_tab(1, RET_DV), state_in],
        out_specs=[pl.BlockSpec((1, chunk, RET_V), lambda bi, c: (bi, c, 0)), state_out],
        out_shape=[jax.ShapeDtypeStruct((b, l, RET_V), proj.dtype),
                   jax.ShapeDtypeStruct((b, RET_HEADS, RET_DK, RET_DV), F32)],
        scratch_shapes=scratch,
        compiler_params=_params(*semantics),
        name="retention",
    )(proj, proj, proj, proj, *tables, s0)


def _moba_prompt_kernel(q_ref, k_ref, v_ref, o_ref, *, nb, tile_blocks):
    k = k_ref[0]
    s_len = k.shape[0]
    means = jnp.sum(k.reshape(nb, MOBA_BLOCK, ATTN_DH), axis=1) * (1.0 / MOBA_BLOCK)
    means = _pad_rows(means, -(-nb // 8) * 8)
    key_lane = lax.broadcasted_iota(jnp.int32, (s_len, 128), 1)
    key_block = lax.broadcasted_iota(jnp.int32, (s_len, 128), 0) // MOBA_BLOCK
    kb = jnp.concatenate([k.astype(BF16), jnp.where(key_block == key_lane, 1.0, 0.0).astype(BF16)], axis=1)
    vb = jnp.concatenate([v_ref[0].astype(BF16), jnp.where(key_lane == 0, 1.0, 0.0).astype(BF16)], axis=1)
    tile = tile_blocks * MOBA_BLOCK
    n_tiles = nb // tile_blocks
    blk = lax.broadcasted_iota(jnp.int32, (means.shape[0], tile), 0)
    own_blk = lax.broadcasted_iota(jnp.int32, (means.shape[0], tile), 1) // MOBA_BLOCK
    row = lax.broadcasted_iota(jnp.int32, (tile, tile), 0)
    col = lax.broadcasted_iota(jnp.int32, (tile, tile), 1)
    causal = col <= row
    eye = jnp.where(lax.broadcasted_iota(jnp.int32, (MOBA_BLOCK, MOBA_BLOCK), 0)
                    == lax.broadcasted_iota(jnp.int32, (MOBA_BLOCK, MOBA_BLOCK), 1), 1.0, 0.0).astype(BF16)

    def masked_logits(i):
        q = q_ref[0, i * tile:(i + 1) * tile, :]
        first = i * tile_blocks
        bias = jnp.zeros((tile, 128), BF16)
        if first + tile_blocks > 1:
            s = lax.dot_general(means, q, _NT, preferred_element_type=F32, precision=lax.Precision.HIGHEST)
            elig = blk < own_blk + first
            sm = jnp.where(elig, s, NEG)
            rank = jnp.zeros(sm.shape, F32)
            for m in range(first + tile_blocks - 1):
                cm = sm[m:m + 1, :]
                beats = jnp.where(cm > sm, 1.0, jnp.where(cm == sm, jnp.where(blk > m, 1.0, 0.0), 0.0))
                rank = rank + beats
            keep = jnp.where(rank < MOBA_TOPK, jnp.where(sm > 0.5 * NEG, 0.0, NEG), NEG)
            bias_t = _pad_rows(jnp.where(elig, keep, 0.0), 128)
            bias = jnp.concatenate([
                lax.dot_general(eye, bias_t[:, j * MOBA_BLOCK:(j + 1) * MOBA_BLOCK].astype(BF16), _NT,
                                preferred_element_type=F32) for j in range(tile_blocks)], axis=0).astype(BF16)
        qa = jnp.concatenate([(q * (ATTN_DH ** -0.5 * LOG2_E)).astype(BF16), bias], axis=1)
        kv_rows = (i + 1) * tile
        logits = lax.dot_general(qa, kb[:kv_rows], _NT, preferred_element_type=F32)
        own = jnp.where(causal, logits[:, i * tile:], NEG)
        return own if i == 0 else jnp.concatenate([logits[:, :i * tile], own], axis=1)

    def attend(i, logits):
        p = jnp.exp2(logits - jnp.max(logits, axis=-1, keepdims=True))
        acc = jnp.dot(p.astype(BF16), vb[:(i + 1) * tile], preferred_element_type=F32)
        o_ref[0, i * tile:(i + 1) * tile, :] = (
            acc[:, :ATTN_DH] / acc[:, ATTN_DH:ATTN_DH + 1]).astype(o_ref.dtype)

    order = list(range(n_tiles - 1, -1, -1))
    logits = masked_logits(order[0])
    for n, i in enumerate(order):
        following = masked_logits(order[n + 1]) if n + 1 < n_tiles else None
        attend(i, logits)
        logits = following


def moba_prompt(q, k, v):
    b, s, _ = q.shape
    assert s % (MOBA_BLOCK * PROMPT_TILE_BLOCKS) == 0
    nb = s // MOBA_BLOCK
    spec = pl.BlockSpec((1, s, ATTN_DH), lambda bi, h: (bi, 0, h))
    return pl.pallas_call(
        functools.partial(_moba_prompt_kernel, nb=nb, tile_blocks=PROMPT_TILE_BLOCKS),
        grid=(b, ATTN_HEADS),
        in_specs=[spec, spec, spec],
        out_specs=spec,
        out_shape=jax.ShapeDtypeStruct((b, s, ATTN_HEADS * ATTN_DH), BF16),
        compiler_params=_params("parallel", "parallel"),
        name="moba_prompt",
    )(q, k, v)


def _block_diag_queries(q):
    l, d = q.shape
    tiled = jnp.concatenate([q] * ATTN_HEADS, axis=0)
    row_head = lax.broadcasted_iota(jnp.int32, tiled.shape, 0) // l
    col_head = lax.broadcasted_iota(jnp.int32, tiled.shape, 1) // ATTN_DH
    return jnp.where(row_head == col_head, tiled, 0.0)


def _page_rows(page_ref):
    heads = [page_ref[pl.ds(h, PAGE_SIZE, stride=ATTN_HEADS), :] for h in range(ATTN_HEADS)]
    return jnp.concatenate(heads, axis=1).astype(BF16)


class _PageStream:
    def __init__(self, pt_ref, cache_refs, buf_ref, sem_ref, n_pages):
        assert n_pages % STEP_PAGES == 0
        self.pt_ref, self.cache_refs, self.buf_ref, self.sem_ref = pt_ref, cache_refs, buf_ref, sem_ref
        self.n_pages = n_pages
        self.span = len(cache_refs) * n_pages
        self.total = pl.num_programs(0) * self.span
        self.base = pl.program_id(0) * self.span

    def _copy(self, cache_ref, page, g):
        slot = g % RING_PAGES
        return pltpu.make_async_copy(cache_ref.at[page], self.buf_ref.at[slot], self.sem_ref.at[slot])

    def _start_group(self, g0):
        seq, within = g0 // self.span, g0 % self.span
        which, first = within // self.n_pages, within % self.n_pages
        for c, cache_ref in enumerate(self.cache_refs):
            @pl.when(which == c)
            def _(cache_ref=cache_ref):
                for j in range(STEP_PAGES):
                    self._copy(cache_ref, self.pt_ref[seq, first + j], g0 + j).start(priority=j % 2)

    def prime(self):
        @pl.when(pl.program_id(0) == 0)
        def _():
            for g0 in range(0, RING_PAGES - STEP_PAGES, STEP_PAGES):
                self._start_group(jnp.int32(g0))

    def group(self, it):
        g0 = self.base + it * STEP_PAGES
        ahead = g0 + (RING_PAGES - STEP_PAGES)

        @pl.when(ahead < self.total)
        def _():
            self._start_group(ahead)

        pages = []
        for j in range(STEP_PAGES):
            self._copy(self.cache_refs[0], 0, g0 + j).wait()
            pages.append(self.buf_ref.at[(g0 + j) % RING_PAGES])
        return pages


def _moba_sample_kernel(pt_ref, q_ref, kn_ref, vn_ref, ck_ref, cv_ref, o_ref,
                        buf_ref, sem_ref, lg_ref, bsum_ref, sel_ref, m_ref, l_ref, acc_ref,
                        *, n_new, n_past_blocks, n_pages):
    stream = _PageStream(pt_ref, (ck_ref, cv_ref), buf_ref, sem_ref, n_pages)
    stream.prime()
    groups = n_pages // STEP_PAGES
    wq = _block_diag_queries(q_ref[0])
    w = (wq * ATTN_DH ** -0.5).astype(BF16)
    rows = wq.shape[0]
    lane = lax.broadcasted_iota(jnp.int32, (rows, 128), 1)

    def keys_body(it, carry):
        pages = stream.group(it)
        sums = [jnp.sum(p[...].reshape(PAGE_SIZE, ATTN_HEADS, ATTN_DH), axis=0) for p in pages]
        keys = jnp.concatenate([_page_rows(p) for p in pages], axis=0)
        logits = lax.dot_general(w, keys, _NT, preferred_element_type=F32)
        for j in range(STEP_BLOCKS):
            blk = it * STEP_BLOCKS + j
            bsum_ref[pl.ds(pl.multiple_of(blk * ATTN_HEADS, ATTN_HEADS), ATTN_HEADS), :] = functools.reduce(
                lambda x, y: x + y, sums[j * PAGES_PER_BLOCK:(j + 1) * PAGES_PER_BLOCK])
            lg_ref[blk] = logits[:, j * MOBA_BLOCK:(j + 1) * MOBA_BLOCK]
        return carry

    lax.fori_loop(0, groups, keys_body, 0)

    kn = kn_ref[0]
    past_sums = jnp.concatenate(
        [bsum_ref[pl.ds(h, n_past_blocks, stride=ATTN_HEADS), :] for h in range(ATTN_HEADS)], axis=1)
    first_row = lax.broadcasted_iota(jnp.int32, (8, kn.shape[1]), 0) == 0
    own_sum = jnp.where(first_row, jnp.sum(kn, axis=0, keepdims=True), 0.0)
    means = _pad_rows(jnp.concatenate([past_sums, own_sum], axis=0) * (1.0 / MOBA_BLOCK), 128)
    s = lax.dot_general(wq, means, _NT, preferred_element_type=F32, precision=lax.Precision.HIGHEST)
    work = jnp.where(lane < n_past_blocks, s, NEG)
    sel = jnp.zeros(work.shape, F32)
    for _ in range(MOBA_TOPK):
        mx = jnp.max(work, axis=-1, keepdims=True)
        first = jnp.min(jnp.where(work == mx, lane, 128), axis=-1, keepdims=True)
        pick = lane == first
        sel = jnp.where(pick, jnp.where(mx > 0.5 * NEG, 1.0, 0.0), sel)
        work = jnp.where(pick, -jnp.inf, work)
    sel_ref[...] = sel
    t = lax.broadcasted_iota(jnp.int32, (rows, 128), 0) % n_new
    lo = lax.dot_general(w, _pad_rows(kn, 128).astype(BF16), _NT, preferred_element_type=F32)
    lo = jnp.where(lane <= t, lo, NEG)
    m0 = jnp.max(lo, axis=-1, keepdims=True)
    p0 = jnp.exp(lo - m0)
    m_ref[...] = jnp.broadcast_to(m0, m_ref.shape)
    l_ref[...] = jnp.broadcast_to(jnp.sum(p0, axis=-1, keepdims=True), l_ref.shape)
    acc_ref[...] = jnp.dot(p0.astype(BF16), _pad_rows(vn_ref[0], 128).astype(BF16), preferred_element_type=F32)

    def values_body(it, carry):
        pages = stream.group(groups + it)
        sel = sel_ref[...]
        lgs = []
        for j in range(STEP_BLOCKS):
            blk = it * STEP_BLOCKS + j
            picked = jnp.max(jnp.where(lane == blk, sel, 0.0), axis=-1, keepdims=True) > 0.5
            lgs.append(jnp.where(picked, lg_ref[blk], NEG))
        m_old = m_ref[:, :1]
        m_new = functools.reduce(jnp.maximum, [m_old] + [jnp.max(lg, axis=-1, keepdims=True) for lg in lgs])
        alpha = jnp.exp(m_old - m_new)
        ps = [jnp.exp(lg - m_new) for lg in lgs]
        m_ref[...] = jnp.broadcast_to(m_new, m_ref.shape)
        l_ref[...] = alpha * l_ref[...] + functools.reduce(
            lambda x, y: x + y, [jnp.sum(p, axis=-1, keepdims=True) for p in ps])
        values = jnp.concatenate([_page_rows(p) for p in pages], axis=0)
        acc_ref[...] = alpha * acc_ref[...] + jnp.dot(
            jnp.concatenate(ps, axis=1).astype(BF16), values, preferred_element_type=F32)
        return carry

    lax.fori_loop(0, groups, values_body, 0)

    out = acc_ref[...] / l_ref[:, :1]
    for h in range(ATTN_HEADS):
        o_ref[0, :, h * ATTN_DH:(h + 1) * ATTN_DH] = (
            out[h * n_new:(h + 1) * n_new, h * ATTN_DH:(h + 1) * ATTN_DH].astype(o_ref.dtype))


def moba_sample(q, k_new, v_new, cache_k, cache_v, page_table):
    b, l, d = q.shape
    n_pages = PAST_LEN // PAGE_SIZE
    assert n_pages % STEP_PAGES == 0 and (PAST_LEN + l - 1) // MOBA_BLOCK == PAST_LEN // MOBA_BLOCK
    nblk = n_pages // PAGES_PER_BLOCK
    rows = ATTN_HEADS * l
    page_rows = PAGE_SIZE * ATTN_HEADS
    ck = cache_k.reshape(-1, page_rows, ATTN_DH)
    cv = cache_v.reshape(-1, page_rows, ATTN_DH)

    per_batch = lambda *shape: pl.BlockSpec((1,) + shape, lambda bi, pt: (bi,) + (0,) * len(shape))
    cache = pl.BlockSpec(memory_space=pl.ANY)
    return pl.pallas_call(
        functools.partial(_moba_sample_kernel, n_new=l, n_past_blocks=nblk, n_pages=n_pages),
        grid_spec=pltpu.PrefetchScalarGridSpec(
            num_scalar_prefetch=1,
            grid=(b,),
            in_specs=[per_batch(l, d), per_batch(l, d), per_batch(l, d), cache, cache],
            out_specs=per_batch(l, d),
            scratch_shapes=[pltpu.VMEM((RING_PAGES, page_rows, ATTN_DH), F32),
                            pltpu.SemaphoreType.DMA((RING_PAGES,)),
                            pltpu.VMEM((nblk, rows, MOBA_BLOCK), F32),
                            pltpu.VMEM((nblk * ATTN_HEADS, ATTN_DH), F32),
                            pltpu.VMEM((rows, 128), F32),
                            pltpu.VMEM((rows, 128), F32),
                            pltpu.VMEM((rows, 128), F32),
                            pltpu.VMEM((rows, d), F32)]),
        out_shape=jax.ShapeDtypeStruct((b, l, d), F32),
        compiler_params=_params("arbitrary"),
        name="moba_sample",
    )(page_table, q, k_new, v_new, ck, cv)


def kernel(x_prompt, x_sample, state_ret, cache_k, cache_v, page_table, ret_norm_g, ret_w_in, ret_w_out,
           attn_norm_g, attn_w_q, attn_w_out, kv_norm_g, w_kv, mlp_norm_g, mlp_w_up, mlp_w_down, final_norm_g):
    bp, lp, d = x_prompt.shape
    bs, ls, _ = x_sample.shape
    assert ret_w_in.shape[0] == 1 and attn_w_q.shape[0] == 1 and mlp_w_up.shape[0] == 2

    def trunk(x, pos, s0, attend):
        b, l, _ = x.shape
        h = x.reshape(b * l, d)
        act = BF16 if l % RET_ROWS == 0 else F32
        proj = norm_matmul(h, ret_norm_g[0], ret_w_in[0], act)
        o, s = retention(proj.reshape(b, l, -1), pos, s0)
        h = matmul_residual(o.reshape(b * l, RET_V).astype(BF16), ret_w_out[0], h)
        h = mlp(h, mlp_norm_g[0], mlp_w_up, mlp_w_down, 0)
        k, v, q = kvq_proj(h, kv_norm_g, attn_norm_g[0], w_kv, attn_w_q[0])
        o = attend(q.reshape(b, l, d), k.reshape(b, l, d), v.reshape(b, l, d))
        h = matmul_residual(o.reshape(b * l, d).astype(BF16), attn_w_out[0], h)
        y = mlp(h, mlp_norm_g[1], mlp_w_up, mlp_w_down, 1, g_final=final_norm_g)
        kv_shape = (b, l, ATTN_HEADS, ATTN_DH)
        return y.reshape(b, l, d), s[None], k.reshape(kv_shape), v.reshape(kv_shape)

    y_p, s_p, k_p, v_p = trunk(x_prompt, 0, None, moba_prompt)
    y_s, s_s, k_s, v_s = trunk(
        x_sample, PAST_LEN, state_ret[0],
        lambda q, k, v: moba_sample(q, k, v, cache_k, cache_v, page_table))
    return (y_p, y_s, s_p, s_s, k_p, v_p, k_s, v_s)
```
